```python
import jax, jax.numpy as jnp
from jax import lax
import numpy as np

D_MODEL = 1024
BATCH = 16
SEQ = 2048
DEPTH = 1

GRID_W = 64
CTX_LEN = 256
HG_HEADS = 4
HG_DK = 128
HG_DV = 128
HG_KEY = HG_HEADS * HG_DK
HG_VAL = HG_HEADS * HG_DV
HG_CHUNK = 64
ATT_HEADS = 8
ATT_KV_HEADS = 2
ATT_HEAD_DIM = 64
ATT_GROUPS = ATT_HEADS // ATT_KV_HEADS
ATT_Q = ATT_HEADS * ATT_HEAD_DIM
ATT_KV = ATT_KV_HEADS * ATT_HEAD_DIM
Q_BLOCK = 128
ROPE_AXIS_DIM = ATT_HEAD_DIM // 2
ROPE_THETA = 10000.0
N_EXPERTS = 16
EC_CAPACITY_FACTOR = 2
EXPERT_FF = 1024
IN_SPLITS = (HG_KEY, HG_KEY, HG_KEY, HG_VAL, HG_VAL, ATT_Q, ATT_KV, ATT_KV, D_MODEL, D_MODEL)
IN_COLS = 3 * HG_KEY + 2 * HG_VAL + ATT_Q + 2 * ATT_KV + 2 * D_MODEL
N_MOD = 6
NORM_EPS = 1e-6

kernel_name = "hybrid_hgrn2_gqa_ec_moe_diffusion_layer"


def rms_norm(x, g):
    xf = x.astype(jnp.float32)
    y = xf * lax.rsqrt(jnp.mean(xf * xf, axis=-1, keepdims=True) + NORM_EPS)
    return (y * g.astype(jnp.float32)).astype(x.dtype)


def modulate(h, shift, scale):
    return h * (1 + scale) + shift


def adaln_params(cond, w_mod, b_mod):
    m = jax.nn.silu(cond) @ w_mod + b_mod
    return jnp.split(m[..., None, :], N_MOD, axis=-1)


def axial_rope_tables(length):
    rows = length // GRID_W
    row = jnp.repeat(jnp.arange(rows, dtype=jnp.float32), GRID_W)
    col = jnp.tile(jnp.arange(GRID_W, dtype=jnp.float32), rows)
    inv_freq = ROPE_THETA ** (-jnp.arange(0, ROPE_AXIS_DIM, 2, dtype=jnp.float32) / ROPE_AXIS_DIM)
    ang = jnp.concatenate([row[:, None] * inv_freq, col[:, None] * inv_freq], axis=-1)
    return jnp.cos(ang), jnp.sin(ang)


def apply_axial_rope(x, cos, sin):
    b_, length, h_, d = x.shape
    half = ROPE_AXIS_DIM // 2
    xs = x.astype(jnp.float32).reshape(b_, length, h_, 2, ROPE_AXIS_DIM)
    x1, x2 = xs[..., :half], xs[..., half:]
    c = cos.reshape(length, 1, 2, half)
    s = sin.reshape(length, 1, 2, half)
    out = jnp.concatenate([x1 * c - x2 * s, x2 * c + x1 * s], axis=-1)
    return out.reshape(b_, length, h_, d).astype(x.dtype)


def heads(a, n):
    return a.reshape(a.shape[0], a.shape[1], n, -1)


def hgrn_gates(f_pre, lb):
    f_pre = heads(f_pre.astype(jnp.float32), HG_HEADS)
    lb = lb.reshape(HG_HEADS, HG_DK)
    log_f = jnp.log(lb + (1 - lb) * jax.nn.sigmoid(f_pre))
    key = (1 - lb) * jax.nn.sigmoid(-f_pre)
    return key, log_f


def hgrn_chunk_scan(q, k, v, log_f, s0):
    b_, length, h_, _ = q.shape
    dv = v.shape[-1]
    n_chunks = length // HG_CHUNK

    def chunks(a):
        return a.astype(jnp.float32).reshape(b_, n_chunks, HG_CHUNK, h_, a.shape[-1]).transpose(1, 0, 3, 2, 4)

    incl = jnp.tril(jnp.ones((HG_CHUNK, HG_CHUNK), dtype=bool))[:, :, None]

    def step(state, inp):
        qc, kc, vc, fc = inp
        cum = jnp.cumsum(fc, axis=2)
        rel = jnp.where(incl, cum[:, :, :, None, :] - cum[:, :, None, :, :], -jnp.inf)
        scores = jnp.einsum("bhtk,bhsk,bhtsk->bhts", qc, kc, jnp.exp(rel))
        o = (jnp.einsum("bhts,bhsv->bhtv", scores, vc)
             + jnp.einsum("bhtk,bhkv->bhtv", qc * jnp.exp(cum), state))
        last = cum[:, :, -1:, :]
        new_state = (jnp.exp(last[:, :, 0, :, None]) * state
                     + jnp.einsum("bhsk,bhsv->bhkv", kc * jnp.exp(last - cum), vc))
        return new_state, o

    s_fin, o = lax.scan(step, s0.astype(jnp.float32), tuple(chunks(a) for a in (q, k, v, log_f)))
    o = o.transpose(1, 0, 3, 2, 4).reshape(b_, length, h_, dv)
    return o.astype(v.dtype), s_fin


def hgrn_bidir(q, i, k_f, lf_f, k_b, lf_b, s_f0, s_b0):
    o_f, s_f = hgrn_chunk_scan(q, k_f, i, lf_f, s_f0)
    flip = lambda a: a[:, ::-1]
    o_b, s_b = hgrn_chunk_scan(flip(q), flip(k_b), flip(i), flip(lf_b), s_b0)
    return o_f + flip(o_b), s_f, s_b


def attend_blocks(q, k, v):
    b_, lq = q.shape[:2]
    nb = lq // Q_BLOCK
    qb = q.reshape(b_, nb, Q_BLOCK, ATT_KV_HEADS, ATT_GROUPS, ATT_HEAD_DIM).transpose(1, 0, 2, 3, 4, 5)
    scale = ATT_HEAD_DIM ** -0.5

    def one_block(qi):
        s = jnp.einsum("bqhgd,bkhd->bhgqk", qi, k, preferred_element_type=jnp.float32) * scale
        p = jax.nn.softmax(s, axis=-1)
        return jnp.einsum("bhgqk,bkhd->bqhgd", p.astype(v.dtype), v)

    o = lax.map(one_block, qb)
    return o.transpose(1, 0, 2, 3, 4, 5).reshape(b_, lq, ATT_Q)


def token_mix(hx, hc, w_in, lb_f, lb_b, hg_norm_g, q_norm_g, k_norm_g,
              w_branch_a, w_branch_b, w_out, cos, sin, ctx_out):
    split_at = np.cumsum(IN_SPLITS)[:-1].tolist()

    def project(h):
        return jnp.split(h @ w_in, split_at, axis=-1)

    def qk_norm(aq, ak):
        return rms_norm(heads(aq, ATT_HEADS), q_norm_g), rms_norm(heads(ak, ATT_KV_HEADS), k_norm_g)

    def merge(o_hg, g_hg, o_att, ga, gb):
        b_, length = g_hg.shape[:2]
        a = (rms_norm(o_hg, hg_norm_g.reshape(HG_HEADS, HG_DV)).reshape(b_, length, HG_VAL)
             * jax.nn.silu(g_hg)) @ w_branch_a
        bb = o_att @ w_branch_b
        return (jax.nn.sigmoid(ga) * a + jax.nn.sigmoid(gb) * bb) @ w_out

    cq, cff, cfb, ci, cg, caq, cak, cav, cga, cgb = project(hc)
    ckf, clf = hgrn_gates(cff, lb_f)
    ckb, clb = hgrn_gates(cfb, lb_b)
    zeros = jnp.zeros((hc.shape[0], HG_HEADS, HG_DK, HG_DV), jnp.float32)
    co, cs_f, cs_b = hgrn_bidir(heads(cq, HG_HEADS), heads(ci, HG_HEADS), ckf, clf, ckb, clb, zeros, zeros)
    cqa, cka = qk_norm(caq, cak)
    cva = heads(cav, ATT_KV_HEADS)

    xq, xff, xfb, xi, xg, xaq, xak, xav, xga, xgb = project(hx)
    xkf, xlf = hgrn_gates(xff, lb_f)
    xkb, xlb = hgrn_gates(xfb, lb_b)
    xo, _, _ = hgrn_bidir(heads(xq, HG_HEADS), heads(xi, HG_HEADS), xkf, xlf, xkb, xlb, cs_f, cs_b)
    xqa, xka = qk_norm(xaq, xak)
    xqa = apply_axial_rope(xqa, cos, sin)
    xka = apply_axial_rope(xka, cos, sin)
    keys = jnp.concatenate([cka, xka], axis=1)
    vals = jnp.concatenate([cva, heads(xav, ATT_KV_HEADS)], axis=1)
    y_x = merge(xo, xg, attend_blocks(xqa, keys, vals), xga, xgb)
    if not ctx_out:
        return y_x, None
    y_c = merge(co, cg, attend_blocks(cqa, cka, cva), cga, cgb)
    return y_x, y_c


def expert_choice_moe(h, w_router, w_gate, w_up, w_down):
    b_, length, _ = h.shape
    cap = EC_CAPACITY_FACTOR * length // N_EXPERTS
    aff = jax.nn.softmax((h @ w_router).astype(jnp.float32), axis=-1)
    gate, idx = lax.top_k(jnp.swapaxes(aff, 1, 2), cap)
    bidx = jnp.arange(b_)[:, None, None]
    xe = h[bidx, idx]
    hid = jax.nn.silu(jnp.einsum("becd,edf->becf", xe, w_gate)) * jnp.einsum("becd,edf->becf", xe, w_up)
    ye = jnp.einsum("becf,efd->becd", hid, w_down) * gate[..., None].astype(h.dtype)
    return jnp.zeros_like(h).at[bidx, idx].add(ye)


def setup_inputs(seed: int = 0) -> dict:
    key = jax.random.key(seed)
    ks = jax.random.split(key, 24)
    n = jax.random.normal
    f32 = jnp.float32
    d = D_MODEL
    return {
        "x": n(ks[0], (BATCH, SEQ, d), f32),
        "c": n(ks[1], (BATCH, d), f32),
        "ctx": n(ks[2], (BATCH, CTX_LEN, d), f32),
        "c_ctx": n(ks[3], (d,), f32),
        "w_mod": n(ks[4], (DEPTH, d, N_MOD * d), f32) * (0.5 * d ** -0.5),
        "b_mod": n(ks[5], (DEPTH, N_MOD * d), f32) * 0.02,
        "norm1_g": 1.0 + 0.1 * n(ks[6], (DEPTH, d), f32),
        "norm2_g": 1.0 + 0.1 * n(ks[7], (DEPTH, d), f32),
        "w_in": n(ks[8], (DEPTH, d, IN_COLS), f32) * d ** -0.5,
        "hg_lb_logits": 0.5 * n(ks[9], (2, DEPTH + 1, HG_KEY), f32),
        "hg_norm_g": 1.0 + 0.1 * n(ks[10], (DEPTH, HG_VAL), f32),
        "q_norm_g": 1.0 + 0.1 * n(ks[11], (DEPTH, ATT_HEAD_DIM), f32),
        "k_norm_g": 1.0 + 0.1 * n(ks[12], (DEPTH, ATT_HEAD_DIM), f32),
        "w_branch_a": n(ks[13], (DEPTH, HG_VAL, d), f32) * HG_VAL ** -0.5,
        "w_branch_b": n(ks[14], (DEPTH, ATT_Q, d), f32) * ATT_Q ** -0.5,
        "w_out": n(ks[15], (DEPTH, d, d), f32) * d ** -0.5,
        "w_router": n(ks[16], (DEPTH, d, N_EXPERTS), f32) * d ** -0.5,
        "w_exp_gate": n(ks[17], (DEPTH, N_EXPERTS, d, EXPERT_FF), f32) * d ** -0.5,
        "w_exp_up": n(ks[18], (DEPTH, N_EXPERTS, d, EXPERT_FF), f32) * d ** -0.5,
        "w_exp_down": n(ks[19], (DEPTH, N_EXPERTS, EXPERT_FF, d), f32) * EXPERT_FF ** -0.5,
        "final_norm_g": 1.0 + 0.1 * n(ks[20], (d,), f32),
    }


def reference(x, c, ctx, c_ctx, w_mod, b_mod, norm1_g, norm2_g, w_in, hg_lb_logits, hg_norm_g,
              q_norm_g, k_norm_g, w_branch_a, w_branch_b, w_out, w_router, w_exp_gate, w_exp_up,
              w_exp_down, final_norm_g):
    cos, sin = axial_rope_tables(x.shape[1])
    lb_all = jnp.cumsum(jax.nn.softmax(hg_lb_logits.astype(jnp.float32), axis=1), axis=1)
    for l in range(DEPTH):
        last = l == DEPTH - 1
        sh1, sc1, g1, sh2, sc2, g2 = adaln_params(c, w_mod[l], b_mod[l])
        csh1, csc1, cg1, csh2, csc2, cg2 = adaln_params(c_ctx, w_mod[l], b_mod[l])
        hx = modulate(rms_norm(x, norm1_g[l]), sh1, sc1)
        hc = modulate(rms_norm(ctx, norm1_g[l]), csh1, csc1)
        y_x, y_c = token_mix(hx, hc, w_in[l], lb_all[0, l], lb_all[1, l], hg_norm_g[l], q_norm_g[l],
                             k_norm_g[l], w_branch_a[l], w_branch_b[l], w_out[l], cos, sin, not last)
        x = x + g1 * y_x
        x = x + g2 * expert_choice_moe(modulate(rms_norm(x, norm2_g[l]), sh2, sc2),
                                       w_router[l], w_exp_gate[l], w_exp_up[l], w_exp_down[l])
        if not last:
            ctx = ctx + cg1 * y_c
            ctx = ctx + cg2 * expert_choice_moe(modulate(rms_norm(ctx, norm2_g[l]), csh2, csc2),
                                                w_router[l], w_exp_gate[l], w_exp_up[l], w_exp_down[l])
    return rms_norm(x, final_norm_g)
```

```python
import functools

import numpy as np
import jax
import jax.numpy as jnp
from jax import lax
from jax.experimental import pallas as pl
from jax.experimental.pallas import tpu as pltpu

F32 = jnp.float32
BF16 = jnp.bfloat16
I32 = jnp.int32

GRID_W = 64
HG_HEADS = 4
HG_DK = 128
HG_DV = 128
HG_KEY = HG_HEADS * HG_DK
HG_VAL = HG_HEADS * HG_DV
ATT_HEADS = 8
ATT_KV_HEADS = 2
ATT_HEAD_DIM = 64
ATT_GROUPS = ATT_HEADS // ATT_KV_HEADS
ATT_Q = ATT_HEADS * ATT_HEAD_DIM
ATT_KV = ATT_KV_HEADS * ATT_HEAD_DIM
ROPE_AXIS_DIM = ATT_HEAD_DIM // 2
ROPE_THETA = 10000.0
N_EXPERTS = 16
EC_CAPACITY_FACTOR = 2
N_MOD = 6
NORM_EPS = 1e-6

LANES = 128
SUBLANES = 8
VMEM_LIMIT_BYTES = 56 * 1024 * 1024

HG_CHUNK = 64
TOKEN_TILE = 512
CTX_TILE = 256
Q_TILE = 256
FFN_BATCH_GROUP = 4

_OFF_HQ = 0
_OFF_FF = _OFF_HQ + HG_KEY
_OFF_FB = _OFF_FF + HG_KEY
_OFF_I = _OFF_FB + HG_KEY
_OFF_G = _OFF_I + HG_VAL
_OFF_AQ = _OFF_G + HG_VAL
_OFF_AK = _OFF_AQ + ATT_Q
_OFF_AV = _OFF_AK + ATT_KV


def _params(sem):
    return pltpu.CompilerParams(dimension_semantics=sem, vmem_limit_bytes=VMEM_LIMIT_BYTES)


def _const_spec(shape):
    zeros = (0,) * len(shape)
    return pl.BlockSpec(shape, lambda *_: zeros)


def _adaln_kernel(c_ref, w_ref, b_ref, o_ref):
    c = c_ref[...]
    a = c * jax.nn.sigmoid(c)
    o_ref[...] = jnp.dot(a, w_ref[...], preferred_element_type=F32,
                         precision=lax.Precision.HIGHEST) + b_ref[...]


def _adaln(cond, w_mod, b_mod):
    n, d = cond.shape
    ncol = w_mod.shape[1]
    return pl.pallas_call(
        _adaln_kernel,
        grid=(ncol // d,),
        in_specs=[pl.BlockSpec((n, d), lambda j: (0, 0)),
                  pl.BlockSpec((d, d), lambda j: (0, j)),
                  pl.BlockSpec((1, d), lambda j: (0, j))],
        out_specs=pl.BlockSpec((n, d), lambda j: (0, j)),
        out_shape=jax.ShapeDtypeStruct((n, ncol), F32),
        compiler_params=_params(("arbitrary",)),
        name="adaln",
    )(cond, w_mod, b_mod.reshape(1, ncol))


def _rms_mod(x, gain, shift, scale):
    ms = jnp.mean(x * x, axis=-1, keepdims=True)
    h = x * lax.rsqrt(ms + NORM_EPS) * gain
    return h * (1.0 + scale) + shift


def _lower_bound(lbl_ref, direction, layer):
    lg = lbl_ref[direction]
    e = jnp.exp(lg - jnp.max(lg, axis=0, keepdims=True))
    p = e / jnp.sum(e, axis=0, keepdims=True)
    return jnp.sum(p[:layer + 1], axis=0, keepdims=True)


def _hgrn_gate(f_pre, lb):
    s = jax.nn.sigmoid(f_pre)
    return (1.0 - lb) * (1.0 - s), jnp.log(lb + (1.0 - lb) * s)


def _head_rms(a, gsum, gain):
    ssq = jnp.dot((a * a).astype(BF16), gsum, preferred_element_type=F32)
    return a * lax.rsqrt(ssq * (1.0 / ATT_HEAD_DIM) + NORM_EPS) * gain


def _rope(a, cos, sin):
    lane = lax.broadcasted_iota(I32, (1, LANES), 1)
    first = (lane % ROPE_AXIS_DIM) < (ROPE_AXIS_DIM // 2)
    half = ROPE_AXIS_DIM // 2
    outs = []
    for j in range(a.shape[1] // LANES):
        xg = a[:, j * LANES:(j + 1) * LANES]
        partner = jnp.where(first, pltpu.roll(xg, LANES - half, 1), pltpu.roll(xg, half, 1))
        outs.append(xg * cos + partner * sin)
    return outs[0] if len(outs) == 1 else jnp.concatenate(outs, axis=1)


def _inproj_latent_kernel(x_ref, mod_ref, g_ref, w_ref, lbl_ref, qg_ref, kg_ref, cos_ref, sin_ref,
                          gs_ref, hq_ref, kf_ref, lf_ref, kb_ref, lb_ref, vi_ref, sg_ref,
                          aq_ref, ak_ref, av_ref, ga_ref, gb_ref, *, layer):
    hb = _rms_mod(x_ref[...], g_ref[...], mod_ref[0:1, :], mod_ref[1:2, :]).astype(BF16)

    def proj(lo, n):
        return jnp.dot(hb, w_ref[:, lo:lo + n], preferred_element_type=F32)

    hq_ref[...] = proj(_OFF_HQ, HG_KEY).astype(BF16)
    key, logf = _hgrn_gate(proj(_OFF_FF, HG_KEY), _lower_bound(lbl_ref, 0, layer))
    kf_ref[...] = key.astype(BF16)
    lf_ref[...] = logf
    key, logf = _hgrn_gate(proj(_OFF_FB, HG_KEY), _lower_bound(lbl_ref, 1, layer))
    kb_ref[...] = key.astype(BF16)
    lb_ref[...] = logf
    vi_ref[...] = proj(_OFF_I, HG_VAL).astype(BF16)
    g = proj(_OFF_G, HG_VAL)
    sg_ref[...] = (g * jax.nn.sigmoid(g)).astype(BF16)
    cos = cos_ref[...]
    sin = sin_ref[...]
    aq = _head_rms(proj(_OFF_AQ, ATT_Q), gs_ref[...], qg_ref[...])
    aq_ref[...] = (_rope(aq, cos, sin) * (ATT_HEAD_DIM ** -0.5)).astype(BF16)
    ak = _head_rms(proj(_OFF_AK, ATT_KV), gs_ref[0:ATT_KV, 0:ATT_KV], kg_ref[...])
    ak_ref[...] = _rope(ak, cos, sin).astype(BF16)
    av_ref[...] = proj(_OFF_AV, ATT_KV).astype(BF16)
    d = ga_ref.shape[-1]
    ga_ref[...] = jax.nn.sigmoid(proj(_OFF_AV + ATT_KV, d)).astype(BF16)
    gb_ref[...] = jax.nn.sigmoid(proj(_OFF_AV + ATT_KV + d, d)).astype(BF16)


def _inproj_context_kernel(x_ref, mod_ref, g_ref, w_ref, lbl_ref, kg_ref, gs_ref,
                           kf_ref, lf_ref, kb_ref, lb_ref, vi_ref, ak_ref, av_ref, *, layer):
    hb = _rms_mod(x_ref[...], g_ref[...], mod_ref[0:1, :], mod_ref[1:2, :]).astype(BF16)

    def proj(lo, n):
        return jnp.dot(hb, w_ref[:, lo:lo + n], preferred_element_type=F32)

    key, logf = _hgrn_gate(proj(0, HG_KEY), _lower_bound(lbl_ref, 0, layer))
    kf_ref[...] = key.astype(BF16)
    lf_ref[...] = logf
    key, logf = _hgrn_gate(proj(HG_KEY, HG_KEY), _lower_bound(lbl_ref, 1, layer))
    kb_ref[...] = key.astype(BF16)
    lb_ref[...] = logf
    vi_ref[...] = proj(2 * HG_KEY, HG_VAL).astype(BF16)
    ak = _head_rms(proj(2 * HG_KEY + HG_VAL, ATT_KV), gs_ref[...], kg_ref[...])
    ak_ref[...] = ak.astype(BF16)
    av_ref[...] = proj(2 * HG_KEY + HG_VAL + ATT_KV, ATT_KV).astype(BF16)


def _inproj_latent(x, mods, norm_g, w_bf, lbl, qg, kg, cos, sin, gsum, layer):
    b, seq, d = x.shape
    tm = TOKEN_TILE
    tok = lambda n: pl.BlockSpec((None, tm, n), lambda i, t: (i, t, 0))
    shp = lambda n, dt: jax.ShapeDtypeStruct((b, seq, n), dt)
    return pl.pallas_call(
        functools.partial(_inproj_latent_kernel, layer=layer),
        grid=(b, seq // tm),
        in_specs=[tok(d),
                  pl.BlockSpec((None, N_MOD, d), lambda i, t: (i, 0, 0)),
                  _const_spec((1, d)),
                  _const_spec(w_bf.shape),
                  _const_spec(lbl.shape),
                  _const_spec(qg.shape),
                  _const_spec(kg.shape),
                  pl.BlockSpec((tm, LANES), lambda i, t: (t, 0)),
                  pl.BlockSpec((tm, LANES), lambda i, t: (t, 0)),
                  _const_spec(gsum.shape)],
        out_specs=[tok(HG_KEY), tok(HG_KEY), tok(HG_KEY), tok(HG_KEY), tok(HG_KEY), tok(HG_VAL),
                   tok(HG_VAL), tok(ATT_Q), tok(ATT_KV), tok(ATT_KV), tok(d), tok(d)],
        out_shape=[shp(HG_KEY, BF16), shp(HG_KEY, BF16), shp(HG_KEY, F32), shp(HG_KEY, BF16),
                   shp(HG_KEY, F32), shp(HG_VAL, BF16), shp(HG_VAL, BF16), shp(ATT_Q, BF16),
                   shp(ATT_KV, BF16), shp(ATT_KV, BF16), shp(d, BF16), shp(d, BF16)],
        compiler_params=_params(("parallel", "parallel")),
        name="inproj_latent",
    )(x, mods, norm_g, w_bf, lbl, qg, kg, cos, sin, gsum)


def _inproj_context(ctx, mods, ctx_row, norm_g, w_bf, lbl, kg, gsum, layer):
    b, n_ctx, d = ctx.shape
    tm = CTX_TILE
    tok = lambda n: pl.BlockSpec((None, tm, n), lambda i, t: (i, t, 0))
    shp = lambda n, dt: jax.ShapeDtypeStruct((b, n_ctx, n), dt)
    return pl.pallas_call(
        functools.partial(_inproj_context_kernel, layer=layer),
        grid=(b, n_ctx // tm),
        in_specs=[tok(d),
                  pl.BlockSpec((None, N_MOD, d), lambda i, t: (ctx_row, 0, 0)),
                  _const_spec((1, d)),
                  _const_spec(w_bf.shape),
                  _const_spec(lbl.shape),
                  _const_spec(kg.shape),
                  _const_spec(gsum.shape)],
        out_specs=[tok(HG_KEY), tok(HG_KEY), tok(HG_KEY), tok(HG_KEY), tok(HG_VAL),
                   tok(ATT_KV), tok(ATT_KV)],
        out_shape=[shp(HG_KEY, BF16), shp(HG_KEY, F32), shp(HG_KEY, BF16), shp(HG_KEY, F32),
                   shp(HG_VAL, BF16), shp(ATT_KV, BF16), shp(ATT_KV, BF16)],
        compiler_params=_params(("parallel", "parallel")),
        name="inproj_context",
    )(ctx, mods, norm_g, w_bf, lbl, kg, gsum)


_HG_LEVELS = tuple(2 ** i for i in range(int(np.log2(HG_CHUNK))))


def _level_exponent(h, g, cum, cum_ref, reverse):
    c = HG_CHUNK
    row = lax.broadcasted_iota(I32, (c, 1), 0)
    if h == 1:
        query = (row % 2 == 0) if reverse else (row % 2 == 1)
        return jnp.where(query, g, 0.0)
    if h == 2:
        u = row % 4
        up = pltpu.roll(g, c - 1, 0)
        down = pltpu.roll(g, 1, 0)
        if reverse:
            return jnp.where(u == 0, g + up, jnp.where(u == 1, g, jnp.where(u == 2, 0.0, down)))
        return jnp.where(u == 0, up, jnp.where(u == 1, 0.0, jnp.where(u == 2, g, g + down)))
    pieces = []
    for p in range(c // (2 * h)):
        r = p * 2 * h + (h if reverse else h - 1)
        pieces.append(jnp.broadcast_to(cum_ref[r:r + 1, :], (2 * h, HG_DK)))
    mid = pieces[0] if len(pieces) == 1 else jnp.concatenate(pieces, axis=0)
    return -jnp.abs(cum - mid)


def _hgrn_chunk(q, k, v, g, st_ref, cum_ref, tri, level_id, reverse):
    c = HG_CHUNK
    g_hi = g.astype(BF16)
    g_lo = (g - g_hi.astype(F32)).astype(BF16)
    cum = (jnp.dot(tri, g_hi, preferred_element_type=F32)
           + jnp.dot(tri, g_lo, preferred_element_type=F32))
    last = cum[0:1, :] if reverse else cum[c - 1:c, :]
    k32 = k.astype(F32)
    kl = (k32 * jnp.exp(last - cum)).astype(BF16)
    st = st_ref[...]
    st_ref[...] = st * jnp.exp(last) + lax.dot_general(
        v, kl, (((0,), (0,)), ((), ())), preferred_element_type=F32)
    if q is None:
        return None
    q32 = q.astype(F32)
    cum_ref[...] = cum
    nt = (((1,), (1,)), ((), ()))
    scores = jnp.zeros((c, c), F32)
    for li, h in enumerate(_HG_LEVELS):
        a = jnp.exp(_level_exponent(h, g, cum, cum_ref, reverse))
        s_h = lax.dot_general((q32 * a).astype(BF16), (k32 * a).astype(BF16), nt,
                              preferred_element_type=F32)
        scores = jnp.where(level_id == li, s_h, scores)
    self_term = jnp.sum(q32 * k32, axis=-1, keepdims=True)
    scores = jnp.where(level_id == len(_HG_LEVELS), self_term, scores)
    o = jnp.dot(scores.astype(BF16), v, preferred_element_type=F32)
    o = o + lax.dot_general((q32 * jnp.exp(cum)).astype(BF16), st.astype(BF16), nt,
                            preferred_element_type=F32)
    return o


def _hgrn_kernel(q_ref, kf_ref, lf_ref, kb_ref, lb_ref, v_ref,
                 ckf_ref, clf_ref, ckb_ref, clb_ref, cv_ref, gn_ref, o_ref,
                 stf_ref, stb_ref, of_ref, ob_ref, cumf_ref, cumb_ref):
    c = HG_CHUNK
    n_ctx = ckf_ref.shape[0] // c
    n_lat = q_ref.shape[0] // c
    row = lax.broadcasted_iota(I32, (c, c), 0)
    col = lax.broadcasted_iota(I32, (c, c), 1)
    tri_f = jnp.where(col <= row, 1.0, 0.0).astype(BF16)
    tri_b = jnp.where(col >= row, 1.0, 0.0).astype(BF16)
    x = row ^ col
    lvl = jnp.full((c, c), len(_HG_LEVELS), I32)
    for li, h in enumerate(_HG_LEVELS):
        lvl = jnp.where((x >= h) & (x < 2 * h), li, lvl)
    lvl_f = jnp.where(col <= row, lvl, -1)
    lvl_b = jnp.where(col >= row, lvl, -1)

    stf_ref[...] = jnp.zeros_like(stf_ref)
    stb_ref[...] = jnp.zeros_like(stb_ref)

    def ctx_body(j, carry):
        a = pl.multiple_of(j * c, c)
        z = pl.multiple_of((n_ctx - 1 - j) * c, c)
        _hgrn_chunk(None, ckf_ref[pl.ds(a, c), :], cv_ref[pl.ds(a, c), :], clf_ref[pl.ds(a, c), :],
                    stf_ref, cumf_ref, tri_f, lvl_f, False)
        _hgrn_chunk(None, ckb_ref[pl.ds(z, c), :], cv_ref[pl.ds(z, c), :], clb_ref[pl.ds(z, c), :],
                    stb_ref, cumb_ref, tri_b, lvl_b, True)
        return carry

    lax.fori_loop(0, n_ctx, ctx_body, 0)

    def lat_body(j, carry):
        a = pl.multiple_of(j * c, c)
        z = pl.multiple_of((n_lat - 1 - j) * c, c)
        of_ref[pl.ds(a, c), :] = _hgrn_chunk(
            q_ref[pl.ds(a, c), :], kf_ref[pl.ds(a, c), :], v_ref[pl.ds(a, c), :],
            lf_ref[pl.ds(a, c), :], stf_ref, cumf_ref, tri_f, lvl_f, False)
        ob_ref[pl.ds(z, c), :] = _hgrn_chunk(
            q_ref[pl.ds(z, c), :], kb_ref[pl.ds(z, c), :], v_ref[pl.ds(z, c), :],
            lb_ref[pl.ds(z, c), :], stb_ref, cumb_ref, tri_b, lvl_b, True)
        return carry

    lax.fori_loop(0, n_lat, lat_body, 0)

    o = of_ref[...] + ob_ref[...]
    ms = jnp.mean(o * o, axis=-1, keepdims=True)
    o_ref[...] = (o * lax.rsqrt(ms + NORM_EPS) * gn_ref[...]).astype(BF16)


def _hgrn(hq, kf, lf, kb, lb, vi, ckf, clf, ckb, clb, cvi, gn):
    b, seq, _ = hq.shape
    n_ctx = ckf.shape[1]
    lat = pl.BlockSpec((None, seq, HG_DK), lambda i, h: (i, 0, h))
    cx = pl.BlockSpec((None, n_ctx, HG_DK), lambda i, h: (i, 0, h))
    return pl.pallas_call(
        _hgrn_kernel,
        grid=(b, HG_HEADS),
        in_specs=[lat, lat, lat, lat, lat, lat, cx, cx, cx, cx, cx,
                  pl.BlockSpec((1, HG_DV), lambda i, h: (0, h))],
        out_specs=lat,
        out_shape=jax.ShapeDtypeStruct((b, seq, HG_VAL), BF16),
        scratch_shapes=[pltpu.VMEM((HG_DV, HG_DK), F32), pltpu.VMEM((HG_DV, HG_DK), F32),
                        pltpu.VMEM((seq, HG_DV), F32), pltpu.VMEM((seq, HG_DV), F32),
                        pltpu.VMEM((HG_CHUNK, HG_DK), F32), pltpu.VMEM((HG_CHUNK, HG_DK), F32)],
        compiler_params=_params(("parallel", "parallel")),
        name="hgrn_scan",
    )(hq, kf, lf, kb, lb, vi, ckf, clf, ckb, clb, cvi, gn)


def _attn_kernel(q_ref, kx_ref, vx_ref, kc_ref, vc_ref, o_ref, kp_ref, vp_ref):
    kvh = pl.program_id(1)
    width = ATT_GROUPS * ATT_HEAD_DIM

    @pl.when(pl.program_id(2) == 0)
    def _build():
        k_all = jnp.concatenate([kc_ref[...], kx_ref[...]], axis=0)
        v_all = jnp.concatenate([vc_ref[...], vx_ref[...]], axis=0)
        r = lax.broadcasted_iota(I32, (ATT_KV, width), 0)
        cidx = lax.broadcasted_iota(I32, (ATT_KV, width), 1)
        for g in range(ATT_GROUPS):
            place = (cidx // ATT_HEAD_DIM == g) & (r == kvh * ATT_HEAD_DIM + cidx - g * ATT_HEAD_DIM)
            rep = jnp.where(place, 1.0, 0.0).astype(BF16)
            kp_ref[g] = jnp.dot(k_all, rep, preferred_element_type=F32).astype(BF16)
            vp_ref[g] = jnp.dot(v_all, rep, preferred_element_type=F32).astype(BF16)

    q = q_ref[...]
    acc = jnp.zeros(o_ref.shape, F32)
    for g in range(ATT_GROUPS):
        s = lax.dot_general(q, kp_ref[g], (((1,), (1,)), ((), ())), preferred_element_type=F32)
        p = jnp.exp(s - jnp.max(s, axis=-1, keepdims=True))
        inv = 1.0 / jnp.sum(p, axis=-1, keepdims=True)
        acc = acc + jnp.dot(p.astype(BF16), vp_ref[g], preferred_element_type=F32) * inv
    o_ref[...] = acc.astype(BF16)


def _attention(aq, ak, av, cak, cav):
    b, seq, _ = aq.shape
    n_ctx = cak.shape[1]
    width = ATT_GROUPS * ATT_HEAD_DIM
    kx = pl.BlockSpec((None, seq, ATT_KV), lambda i, h, t: (i, 0, 0))
    kc = pl.BlockSpec((None, n_ctx, ATT_KV), lambda i, h, t: (i, 0, 0))
    qo = pl.BlockSpec((None, Q_TILE, width), lambda i, h, t: (i, t, h))
    return pl.pallas_call(
        _attn_kernel,
        grid=(b, ATT_KV_HEADS, seq // Q_TILE),
        in_specs=[qo, kx, kx, kc, kc],
        out_specs=qo,
        out_shape=jax.ShapeDtypeStruct((b, seq, ATT_Q), BF16),
        scratch_shapes=[pltpu.VMEM((ATT_GROUPS, seq + n_ctx, width), BF16),
                        pltpu.VMEM((ATT_GROUPS, seq + n_ctx, width), BF16)],
        compiler_params=_params(("parallel", "parallel", "arbitrary")),
        name="attention",
    )(aq, ak, av, cak, cav)


def _merge_kernel(on_ref, sg_ref, oa_ref, ga_ref, gb_ref, x_ref, mod_ref, n2_ref,
                  wa_ref, wb_ref, wo_ref, wrh_ref, wrl_ref, x1_ref, h2_ref, lg_ref):
    a_in = (on_ref[...].astype(F32) * sg_ref[...].astype(F32)).astype(BF16)
    a = jnp.dot(a_in, wa_ref[...], preferred_element_type=F32)
    bb = jnp.dot(oa_ref[...], wb_ref[...], preferred_element_type=F32)
    m = ga_ref[...].astype(F32) * a + gb_ref[...].astype(F32) * bb
    y = jnp.dot(m.astype(BF16), wo_ref[...], preferred_element_type=F32)
    x1 = x_ref[...] + mod_ref[2:3, :] * y
    x1_ref[...] = x1
    h2 = _rms_mod(x1, n2_ref[...], mod_ref[3:4, :], mod_ref[4:5, :])
    h_hi = h2.astype(BF16)
    h_lo = (h2 - h_hi.astype(F32)).astype(BF16)
    h2_ref[...] = h_hi
    wr_hi = wrh_ref[...]
    logits = (jnp.dot(h_hi, wr_hi, preferred_element_type=F32)
              + jnp.dot(h_lo, wr_hi, preferred_element_type=F32)
              + jnp.dot(h_hi, wrl_ref[...], preferred_element_type=F32))
    lg_ref[...] = logits[:, :N_EXPERTS]


def _merge(on, sg, oatt, sga, sgb, x, mods, norm2_g, wa, wb, wo, wr_hi, wr_lo):
    b, seq, d = x.shape
    tm = TOKEN_TILE
    tok = lambda n: pl.BlockSpec((None, tm, n), lambda i, t: (i, t, 0))
    return pl.pallas_call(
        _merge_kernel,
        grid=(b, seq // tm),
        in_specs=[tok(HG_VAL), tok(HG_VAL), tok(ATT_Q), tok(d), tok(d), tok(d),
                  pl.BlockSpec((None, N_MOD, d), lambda i, t: (i, 0, 0)),
                  _const_spec((1, d)), _const_spec(wa.shape), _const_spec(wb.shape),
                  _const_spec(wo.shape), _const_spec(wr_hi.shape), _const_spec(wr_lo.shape)],
        out_specs=[tok(d), tok(d), tok(N_EXPERTS)],
        out_shape=[jax.ShapeDtypeStruct((b, seq, d), F32),
                   jax.ShapeDtypeStruct((b, seq, d), BF16),
                   jax.ShapeDtypeStruct((b, seq, N_EXPERTS), F32)],
        compiler_params=_params(("parallel", "parallel")),
        name="merge",
    )(on, sg, oatt, sga, sgb, x, mods, norm2_g, wa, wb, wo, wr_hi, wr_lo)


def _route_kernel(lg_ref, pos_ref, gate_ref, *, cap):
    lg = lg_ref[...]
    n_e, length = lg.shape
    e = jnp.exp(lg - jnp.max(lg, axis=0, keepdims=True))
    aff = e / jnp.sum(e, axis=0, keepdims=True)
    bits = lax.bitcast_convert_type(aff, I32)

    def count(mask):
        return jnp.sum(jnp.where(mask, 1.0, 0.0), axis=1, keepdims=True)

    def value_step(i, t):
        cand = t | lax.shift_left(jnp.int32(1), (30 - i).astype(I32))
        return jnp.where(count(bits >= cand) >= cap, cand, t)

    thr = lax.fori_loop(0, 31, value_step, jnp.zeros((n_e, 1), I32))
    above = bits > thr
    tied = bits == thr
    need = cap - count(above)
    idx = lax.broadcasted_iota(I32, (1, length), 1)
    n_bits = int(np.log2(length))

    def index_step(i, j):
        cand = j | lax.shift_left(jnp.int32(1), (n_bits - 1 - i).astype(I32))
        return jnp.where(count(tied & (idx < cand)) < need, cand, j)

    last = lax.fori_loop(0, n_bits, index_step, jnp.zeros((n_e, 1), I32))
    sel = above | (tied & (idx <= last))

    r = lax.broadcasted_iota(I32, (LANES, LANES), 0)
    cc = lax.broadcasted_iota(I32, (LANES, LANES), 1)
    before = jnp.where(r < cc, 1.0, 0.0).astype(BF16)
    sel_b = jnp.where(sel, 1.0, 0.0).astype(BF16)
    offset = jnp.zeros((n_e, 1), F32)
    pieces = []
    for gidx in range(length // LANES):
        blk = sel_b[:, gidx * LANES:(gidx + 1) * LANES]
        pieces.append(jnp.dot(blk, before, preferred_element_type=F32) + offset)
        offset = offset + jnp.sum(blk.astype(F32), axis=1, keepdims=True)
    pos = jnp.concatenate(pieces, axis=1)
    pos_ref[...] = jnp.where(sel, pos.astype(I32), -1)
    gate_ref[...] = jnp.where(sel, aff, 0.0)


def _route(logits_t, cap):
    b, n_e, length = logits_t.shape
    spec = pl.BlockSpec((None, n_e, length), lambda i: (i, 0, 0))
    return pl.pallas_call(
        functools.partial(_route_kernel, cap=cap),
        grid=(b,),
        in_specs=[spec],
        out_specs=[spec, spec],
        out_shape=[jax.ShapeDtypeStruct((b, n_e, length), I32),
                   jax.ShapeDtypeStruct((b, n_e, length), F32)],
        compiler_params=_params(("parallel",)),
        name="route",
    )(logits_t)


def _gather_kernel(pos_ref, h_ref, xe_ref):
    cap = xe_ref.shape[0]
    length = h_ref.shape[0]
    prow = pos_ref[pl.ds(pl.program_id(1), 1), :]
    slot = lax.broadcasted_iota(I32, (cap, length), 0)
    onehot = jnp.where(prow == slot, 1.0, 0.0).astype(BF16)
    xe_ref[...] = jnp.dot(onehot, h_ref[...], preferred_element_type=F32).astype(BF16)


def _gather(pos, h2, cap):
    b, n_e, length = pos.shape
    d = h2.shape[-1]
    return pl.pallas_call(
        _gather_kernel,
        grid=(b, n_e),
        in_specs=[pl.BlockSpec((None, n_e, length), lambda i, e: (i, 0, 0)),
                  pl.BlockSpec((None, length, d), lambda i, e: (i, 0, 0))],
        out_specs=pl.BlockSpec((None, None, cap, d), lambda i, e: (i, e, 0, 0)),
        out_shape=jax.ShapeDtypeStruct((b, n_e, cap, d), BF16),
        compiler_params=_params(("parallel", "arbitrary")),
        name="gather",
    )(pos, h2)


def _ffn_kernel(xe_ref, wg_ref, wu_ref, wd_ref, ye_ref):
    nb, cap, d = xe_ref.shape
    x = xe_ref[...].reshape(nb * cap, d)
    hg = jnp.dot(x, wg_ref[...], preferred_element_type=F32)
    hu = jnp.dot(x, wu_ref[...], preferred_element_type=F32)
    hid = (hg * jax.nn.sigmoid(hg) * hu).astype(BF16)
    ye = jnp.dot(hid, wd_ref[...], preferred_element_type=F32)
    ye_ref[...] = ye.astype(BF16).reshape(nb, cap, d)


def _ffn(xe, wg, wu, wd):
    b, n_e, cap, d = xe.shape
    ff = wg.shape[-1]
    nb = int(np.gcd(b, FFN_BATCH_GROUP))
    tok = pl.BlockSpec((nb, None, cap, d), lambda e, i: (i, e, 0, 0))
    return pl.pallas_call(
        _ffn_kernel,
        grid=(n_e, b // nb),
        in_specs=[tok,
                  pl.BlockSpec((None, d, ff), lambda e, i: (e, 0, 0)),
                  pl.BlockSpec((None, d, ff), lambda e, i: (e, 0, 0)),
                  pl.BlockSpec((None, ff, d), lambda e, i: (e, 0, 0))],
        out_specs=tok,
        out_shape=jax.ShapeDtypeStruct((b, n_e, cap, d), BF16),
        compiler_params=_params(("parallel", "arbitrary")),
        name="expert_ffn",
    )(xe, wg, wu, wd)


def _combine_kernel(pos_ref, gate_ref, ye_ref, x1_ref, mod_ref, fg_ref, o_ref):
    n_e, cap, d = ye_ref.shape
    tm = x1_ref.shape[0]
    pos = pos_ref[...]
    gate = gate_ref[...]
    slot = lax.broadcasted_iota(I32, (tm, cap), 1)
    pieces = [jnp.where(pos[:, e:e + 1] == slot, gate[:, e:e + 1], 0.0).astype(BF16)
              for e in range(n_e)]
    scatter = jnp.concatenate(pieces, axis=1)
    y = jnp.dot(scatter, ye_ref[...].reshape(n_e * cap, d), preferred_element_type=F32)
    x2 = x1_ref[...] + mod_ref[5:6, :] * y
    ms = jnp.mean(x2 * x2, axis=-1, keepdims=True)
    o_ref[...] = x2 * lax.rsqrt(ms + NORM_EPS) * fg_ref[...]


def _combine(pos_t, gate_t, ye, x1, mods, final_g):
    b, seq, d = x1.shape
    n_e, cap = ye.shape[1], ye.shape[2]
    tm = TOKEN_TILE
    tok = lambda n: pl.BlockSpec((None, tm, n), lambda i, t: (i, t, 0))
    return pl.pallas_call(
        _combine_kernel,
        grid=(b, seq // tm),
        in_specs=[tok(n_e), tok(n_e),
                  pl.BlockSpec((None, n_e, cap, d), lambda i, t: (i, 0, 0, 0)),
                  tok(d),
                  pl.BlockSpec((None, N_MOD, d), lambda i, t: (i, 0, 0)),
                  _const_spec((1, d))],
        out_specs=tok(d),
        out_shape=jax.ShapeDtypeStruct((b, seq, d), F32),
        compiler_params=_params(("parallel", "arbitrary")),
        name="combine",
    )(pos_t, gate_t, ye, x1, mods, final_g)


def _rope_tables(length):
    rows = length // GRID_W
    row = jnp.repeat(jnp.arange(rows, dtype=F32), GRID_W)
    col = jnp.tile(jnp.arange(GRID_W, dtype=F32), rows)
    inv_freq = ROPE_THETA ** (-jnp.arange(0, ROPE_AXIS_DIM, 2, dtype=F32) / ROPE_AXIS_DIM)
    half = ROPE_AXIS_DIM // 2
    cos_parts, sin_parts = [], []
    for pos in (row, col):
        ang = pos[:, None] * inv_freq
        cos_parts += [jnp.cos(ang), jnp.cos(ang)]
        sin_parts += [-jnp.sin(ang), jnp.sin(ang)]
    cos = jnp.concatenate(cos_parts, axis=-1)
    sin = jnp.concatenate(sin_parts, axis=-1)
    reps = LANES // ATT_HEAD_DIM
    assert half * 4 == ATT_HEAD_DIM
    return jnp.tile(cos, (1, reps)), jnp.tile(sin, (1, reps))


def kernel(x, c, ctx, c_ctx, w_mod, b_mod, norm1_g, norm2_g, w_in, hg_lb_logits, hg_norm_g,
           q_norm_g, k_norm_g, w_branch_a, w_branch_b, w_out, w_router, w_exp_gate, w_exp_up,
           w_exp_down, final_norm_g):
    b, seq, d = x.shape
    depth = w_mod.shape[0]
    assert depth == 1, "context-stream update between layers is not implemented"
    layer = 0
    cap = EC_CAPACITY_FACTOR * seq // N_EXPERTS

    n_rows = -(-(b + 1) // SUBLANES) * SUBLANES
    cond = jnp.zeros((n_rows, d), F32).at[:b].set(c).at[b].set(c_ctx)
    mods = _adaln(cond, w_mod[layer], b_mod[layer]).reshape(n_rows, N_MOD, d)

    cos, sin = _rope_tables(seq)
    gsum = jnp.asarray(np.kron(np.eye(ATT_HEADS), np.ones((ATT_HEAD_DIM, ATT_HEAD_DIM))), BF16)
    qg = jnp.tile(q_norm_g[layer], ATT_HEADS).reshape(1, ATT_Q)
    kg = jnp.tile(k_norm_g[layer], ATT_KV_HEADS).reshape(1, ATT_KV)
    w_bf = w_in[layer].astype(BF16)
    w_ctx = jnp.concatenate([w_bf[:, _OFF_FF:_OFF_G], w_bf[:, _OFF_AK:_OFF_AV + ATT_KV]], axis=1)
    n1 = norm1_g[layer].reshape(1, d)

    hq, kf, lf, kb, lb, vi, sg, aq, ak, av, sga, sgb = _inproj_latent(
        x, mods, n1, w_bf, hg_lb_logits, qg, kg, cos, sin, gsum, layer)
    ckf, clf, ckb, clb, cvi, cak, cav = _inproj_context(
        ctx, mods, b, n1, w_ctx, hg_lb_logits, kg, gsum[:ATT_KV, :ATT_KV], layer)

    on = _hgrn(hq, kf, lf, kb, lb, vi, ckf, clf, ckb, clb, cvi,
               hg_norm_g[layer].reshape(1, HG_VAL))
    oatt = _attention(aq, ak, av, cak, cav)

    wr = jnp.zeros((d, LANES), F32).at[:, :N_EXPERTS].set(w_router[layer])
    wr_hi = wr.astype(BF16)
    wr_lo = (wr - wr_hi.astype(F32)).astype(BF16)
    x1, h2, logits = _merge(on, sg, oatt, sga, sgb, x, mods, norm2_g[layer].reshape(1, d),
                            w_branch_a[layer].astype(BF16), w_branch_b[layer].astype(BF16),
                            w_out[layer].astype(BF16), wr_hi, wr_lo)

    pos, gate = _route(jnp.swapaxes(logits, 1, 2), cap)
    xe = _gather(pos, h2, cap)
    ye = _ffn(xe, w_exp_gate[layer].astype(BF16), w_exp_up[layer].astype(BF16),
              w_exp_down[layer].astype(BF16))
    return _combine(jnp.swapaxes(pos, 1, 2), jnp.swapaxes(gate, 1, 2), ye, x1, mods,
                    final_norm_g.reshape(1, d))
```

```python
import functools

import numpy as np
import jax
import jax.numpy as jnp
from jax import lax
from jax.experimental import pallas as pl
from jax.experimental.pallas import tpu as pltpu

F32 = jnp.float32
BF16 = jnp.bfloat16
I32 = jnp.int32

GRID_W = 64
HG_HEADS = 4
HG_DK = 128
HG_DV = 128
HG_KEY = HG_HEADS * HG_DK
HG_VAL = HG_HEADS * HG_DV
ATT_HEADS = 8
ATT_KV_HEADS = 2
ATT_HEAD_DIM = 64
ATT_GROUPS = ATT_HEADS // ATT_KV_HEADS
ATT_Q = ATT_HEADS * ATT_HEAD_DIM
ATT_KV = ATT_KV_HEADS * ATT_HEAD_DIM
ROPE_AXIS_DIM = ATT_HEAD_DIM // 2
ROPE_THETA = 10000.0
N_EXPERTS = 16
EC_CAPACITY_FACTOR = 2
N_MOD = 6
NORM_EPS = 1e-6
LOG2_E = float(np.log2(np.e))

LANES = 128
SUBLANES = 8
VMEM_LIMIT_BYTES = 56 * 1024 * 1024

HG_CHUNK = 64
HG_HEADS_PER_STEP = 4
TOKEN_TILE = 512
CTX_TILE = 256
Q_TILE = 512
FFN_BATCH_GROUP = 4
GATHER_EXPERT_GROUP = 4
COMBINE_TILE = 1024

_OFF_HQ = 0
_OFF_FF = _OFF_HQ + HG_KEY
_OFF_FB = _OFF_FF + HG_KEY
_OFF_I = _OFF_FB + HG_KEY
_OFF_G = _OFF_I + HG_VAL
_OFF_AQ = _OFF_G + HG_VAL
_OFF_AK = _OFF_AQ + ATT_Q
_OFF_AV = _OFF_AK + ATT_KV


def _params(sem):
    return pltpu.CompilerParams(dimension_semantics=sem, vmem_limit_bytes=VMEM_LIMIT_BYTES)


def _const_spec(shape):
    zeros = (0,) * len(shape)
    return pl.BlockSpec(shape, lambda *_: zeros)


def _adaln_kernel(c_ref, w_ref, b_ref, o_ref):
    c = c_ref[...]
    a = c * jax.nn.sigmoid(c)
    o_ref[...] = jnp.dot(a, w_ref[...], preferred_element_type=F32,
                         precision=lax.Precision.HIGHEST) + b_ref[...]


def _adaln(cond, w_mod, b_mod):
    n, d = cond.shape
    ncol = w_mod.shape[1]
    return pl.pallas_call(
        _adaln_kernel,
        grid=(ncol // d,),
        in_specs=[pl.BlockSpec((n, d), lambda j: (0, 0)),
                  pl.BlockSpec((d, d), lambda j: (0, j)),
                  pl.BlockSpec((1, d), lambda j: (0, j))],
        out_specs=pl.BlockSpec((n, d), lambda j: (0, j)),
        out_shape=jax.ShapeDtypeStruct((n, ncol), F32),
        compiler_params=_params(("arbitrary",)),
        name="adaln",
    )(cond, w_mod, b_mod.reshape(1, ncol))


def _rms_mod(x, gain, shift, scale):
    ms = jnp.mean(x * x, axis=-1, keepdims=True)
    h = x * lax.rsqrt(ms + NORM_EPS) * gain
    return h * (1.0 + scale) + shift


def _lower_bound(lbl_ref, direction, layer):
    lg = lbl_ref[direction]
    e = jnp.exp(lg - jnp.max(lg, axis=0, keepdims=True))
    p = e / jnp.sum(e, axis=0, keepdims=True)
    return jnp.sum(p[:layer + 1], axis=0, keepdims=True)


def _hgrn_gate(f_pre, lb):
    s = jax.nn.sigmoid(f_pre)
    return (1.0 - lb) * (1.0 - s), jnp.log(lb + (1.0 - lb) * s)


def _head_rms(a, gsum, gain):
    ssq = jnp.dot((a * a).astype(BF16), gsum, preferred_element_type=F32)
    return a * lax.rsqrt(ssq * (1.0 / ATT_HEAD_DIM) + NORM_EPS) * gain


def _rope(a, cos, sin):
    lane = lax.broadcasted_iota(I32, (1, LANES), 1)
    first = (lane % ROPE_AXIS_DIM) < (ROPE_AXIS_DIM // 2)
    half = ROPE_AXIS_DIM // 2
    outs = []
    for j in range(a.shape[1] // LANES):
        xg = a[:, j * LANES:(j + 1) * LANES]
        partner = jnp.where(first, pltpu.roll(xg, LANES - half, 1), pltpu.roll(xg, half, 1))
        outs.append(xg * cos + partner * sin)
    return outs[0] if len(outs) == 1 else jnp.concatenate(outs, axis=1)


def _inproj_latent_kernel(x_ref, mod_ref, g_ref, w_ref, lbl_ref, qg_ref, kg_ref, cos_ref, sin_ref,
                          gs_ref, hq_ref, kf_ref, lf_ref, kb_ref, lb_ref, vi_ref, sg_ref,
                          aq_ref, ak_ref, av_ref, ga_ref, gb_ref, *, layer):
    hb = _rms_mod(x_ref[...], g_ref[...], mod_ref[0:1, :], mod_ref[1:2, :]).astype(BF16)

    def proj(lo, n):
        return jnp.dot(hb, w_ref[:, lo:lo + n], preferred_element_type=F32)

    hq_ref[...] = proj(_OFF_HQ, HG_KEY).astype(BF16)
    key, logf = _hgrn_gate(proj(_OFF_FF, HG_KEY), _lower_bound(lbl_ref, 0, layer))
    kf_ref[...] = key.astype(BF16)
    lf_ref[...] = logf
    key, logf = _hgrn_gate(proj(_OFF_FB, HG_KEY), _lower_bound(lbl_ref, 1, layer))
    kb_ref[...] = key.astype(BF16)
    lb_ref[...] = logf
    vi_ref[...] = proj(_OFF_I, HG_VAL).astype(BF16)
    g = proj(_OFF_G, HG_VAL)
    sg_ref[...] = (g * jax.nn.sigmoid(g)).astype(BF16)
    cos = cos_ref[...]
    sin = sin_ref[...]
    aq = _head_rms(proj(_OFF_AQ, ATT_Q), gs_ref[...], qg_ref[...])
    aq_ref[...] = (_rope(aq, cos, sin) * (ATT_HEAD_DIM ** -0.5 * LOG2_E)).astype(BF16)
    ak = _head_rms(proj(_OFF_AK, ATT_KV), gs_ref[0:ATT_KV, 0:ATT_KV], kg_ref[...])
    ak_ref[...] = _rope(ak, cos, sin).astype(BF16)
    av_ref[...] = proj(_OFF_AV, ATT_KV).astype(BF16)
    d = ga_ref.shape[-1]
    ga_ref[...] = jax.nn.sigmoid(proj(_OFF_AV + ATT_KV, d)).astype(BF16)
    gb_ref[...] = jax.nn.sigmoid(proj(_OFF_AV + ATT_KV + d, d)).astype(BF16)


def _inproj_context_kernel(x_ref, mod_ref, g_ref, w_ref, lbl_ref, kg_ref, gs_ref,
                           kf_ref, lf_ref, kb_ref, lb_ref, vi_ref, ak_ref, av_ref, *, layer):
    hb = _rms_mod(x_ref[...], g_ref[...], mod_ref[0:1, :], mod_ref[1:2, :]).astype(BF16)

    def proj(lo, n):
        return jnp.dot(hb, w_ref[:, lo:lo + n], preferred_element_type=F32)

    key, logf = _hgrn_gate(proj(0, HG_KEY), _lower_bound(lbl_ref, 0, layer))
    kf_ref[...] = key.astype(BF16)
    lf_ref[...] = logf
    key, logf = _hgrn_gate(proj(HG_KEY, HG_KEY), _lower_bound(lbl_ref, 1, layer))
    kb_ref[...] = key.astype(BF16)
    lb_ref[...] = logf
    vi_ref[...] = proj(2 * HG_KEY, HG_VAL).astype(BF16)
    ak = _head_rms(proj(2 * HG_KEY + HG_VAL, ATT_KV), gs_ref[...], kg_ref[...])
    ak_ref[...] = ak.astype(BF16)
    av_ref[...] = proj(2 * HG_KEY + HG_VAL + ATT_KV, ATT_KV).astype(BF16)


def _inproj_latent(x, mods, norm_g, w_bf, lbl, qg, kg, cos, sin, gsum, layer):
    b, seq, d = x.shape
    tm = TOKEN_TILE
    tok = lambda n: pl.BlockSpec((None, tm, n), lambda i, t: (i, t, 0))
    shp = lambda n, dt: jax.ShapeDtypeStruct((b, seq, n), dt)
    return pl.pallas_call(
        functools.partial(_inproj_latent_kernel, layer=layer),
        grid=(b, seq // tm),
        in_specs=[tok(d),
                  pl.BlockSpec((None, N_MOD, d), lambda i, t: (i, 0, 0)),
                  _const_spec((1, d)),
                  _const_spec(w_bf.shape),
                  _const_spec(lbl.shape),
                  _const_spec(qg.shape),
                  _const_spec(kg.shape),
                  pl.BlockSpec((tm, LANES), lambda i, t: (t, 0)),
                  pl.BlockSpec((tm, LANES), lambda i, t: (t, 0)),
                  _const_spec(gsum.shape)],
        out_specs=[tok(HG_KEY), tok(HG_KEY), tok(HG_KEY), tok(HG_KEY), tok(HG_KEY), tok(HG_VAL),
                   tok(HG_VAL), tok(ATT_Q), tok(ATT_KV), tok(ATT_KV), tok(d), tok(d)],
        out_shape=[shp(HG_KEY, BF16), shp(HG_KEY, BF16), shp(HG_KEY, F32), shp(HG_KEY, BF16),
                   shp(HG_KEY, F32), shp(HG_VAL, BF16), shp(HG_VAL, BF16), shp(ATT_Q, BF16),
                   shp(ATT_KV, BF16), shp(ATT_KV, BF16), shp(d, BF16), shp(d, BF16)],
        compiler_params=_params(("parallel", "parallel")),
        name="inproj_latent",
    )(x, mods, norm_g, w_bf, lbl, qg, kg, cos, sin, gsum)


def _inproj_context(ctx, mods, ctx_row, norm_g, w_bf, lbl, kg, gsum, layer):
    b, n_ctx, d = ctx.shape
    tm = CTX_TILE
    tok = lambda n: pl.BlockSpec((None, tm, n), lambda i, t: (i, t, 0))
    shp = lambda n, dt: jax.ShapeDtypeStruct((b, n_ctx, n), dt)
    return pl.pallas_call(
        functools.partial(_inproj_context_kernel, layer=layer),
        grid=(b, n_ctx // tm),
        in_specs=[tok(d),
                  pl.BlockSpec((None, N_MOD, d), lambda i, t: (ctx_row, 0, 0)),
                  _const_spec((1, d)),
                  _const_spec(w_bf.shape),
                  _const_spec(lbl.shape),
                  _const_spec(kg.shape),
                  _const_spec(gsum.shape)],
        out_specs=[tok(HG_KEY), tok(HG_KEY), tok(HG_KEY), tok(HG_KEY), tok(HG_VAL),
                   tok(ATT_KV), tok(ATT_KV)],
        out_shape=[shp(HG_KEY, BF16), shp(HG_KEY, F32), shp(HG_KEY, BF16), shp(HG_KEY, F32),
                   shp(HG_VAL, BF16), shp(ATT_KV, BF16), shp(ATT_KV, BF16)],
        compiler_params=_params(("parallel", "parallel")),
        name="inproj_context",
    )(ctx, mods, norm_g, w_bf, lbl, kg, gsum)


_HG_LEVELS = tuple(2 ** i for i in range(int(np.log2(HG_CHUNK))))


def _level_exponent(h, g, cum, cum_ref, reverse):
    c = HG_CHUNK
    row = lax.broadcasted_iota(I32, (c, 1), 0)
    if h == 1:
        query = (row % 2 == 0) if reverse else (row % 2 == 1)
        return jnp.where(query, g, 0.0)
    if h == 2:
        u = row % 4
        up = pltpu.roll(g, c - 1, 0)
        down = pltpu.roll(g, 1, 0)
        if reverse:
            return jnp.where(u == 0, g + up, jnp.where(u == 1, g, jnp.where(u == 2, 0.0, down)))
        return jnp.where(u == 0, up, jnp.where(u == 1, 0.0, jnp.where(u == 2, g, g + down)))
    pieces = []
    for p in range(c // (2 * h)):
        r = p * 2 * h + (h if reverse else h - 1)
        pieces.append(jnp.broadcast_to(cum_ref[r:r + 1, :], (2 * h, HG_DK)))
    mid = pieces[0] if len(pieces) == 1 else jnp.concatenate(pieces, axis=0)
    return -jnp.abs(cum - mid)


def _hgrn_chunks(chains, tri, level_id, want_o):
    c = HG_CHUNK
    nt = (((1,), (1,)), ((), ()))
    tn = (((0,), (0,)), ((), ()))
    for ch in chains:
        g = ch["g"]
        g_hi = g.astype(BF16)
        g_lo = (g - g_hi.astype(F32)).astype(BF16)
        t = tri[ch["reverse"]]
        ch["cum"] = (jnp.dot(t, g_hi, preferred_element_type=F32)
                     + jnp.dot(t, g_lo, preferred_element_type=F32))
    for ch in chains:
        cum = ch["cum"]
        last = cum[0:1, :] if ch["reverse"] else cum[c - 1:c, :]
        ch["k32"] = ch["k"].astype(F32)
        kl = (ch["k32"] * jnp.exp(last - cum)).astype(BF16)
        ch["st"] = ch["st_ref"][...]
        ch["st_ref"][...] = ch["st"] * jnp.exp(last) + lax.dot_general(
            ch["v"], kl, tn, preferred_element_type=F32)
    if not want_o:
        return None
    for ch in chains:
        ch["q32"] = ch["q"].astype(F32)
        ch["cum_ref"][...] = ch["cum"]
        ch["scores"] = jnp.zeros((c, c), F32)
    for li, h in enumerate(_HG_LEVELS):
        for ch in chains:
            a = jnp.exp(_level_exponent(h, ch["g"], ch["cum"], ch["cum_ref"], ch["reverse"]))
            s_h = lax.dot_general((ch["q32"] * a).astype(BF16), (ch["k32"] * a).astype(BF16), nt,
                                  preferred_element_type=F32)
            ch["scores"] = jnp.where(level_id[ch["reverse"]] == li, s_h, ch["scores"])
    outs = []
    for ch in chains:
        self_term = jnp.sum(ch["q32"] * ch["k32"], axis=-1, keepdims=True)
        scores = jnp.where(level_id[ch["reverse"]] == len(_HG_LEVELS), self_term, ch["scores"])
        o = jnp.dot(scores.astype(BF16), ch["v"], preferred_element_type=F32)
        o = o + lax.dot_general((ch["q32"] * jnp.exp(ch["cum"])).astype(BF16),
                                ch["st"].astype(BF16), nt, preferred_element_type=F32)
        outs.append(o)
    return outs


def _hgrn_kernel(q_ref, kf_ref, lf_ref, kb_ref, lb_ref, v_ref,
                 ckf_ref, clf_ref, ckb_ref, clb_ref, cv_ref, gn_ref, o_ref,
                 st_ref, of_ref, ob_ref, cum_ref):
    c = HG_CHUNK
    n_heads = q_ref.shape[1] // HG_DK
    n_ctx = ckf_ref.shape[0] // c
    n_lat = q_ref.shape[0] // c
    row = lax.broadcasted_iota(I32, (c, c), 0)
    col = lax.broadcasted_iota(I32, (c, c), 1)
    tri = {False: jnp.where(col <= row, 1.0, 0.0).astype(BF16),
           True: jnp.where(col >= row, 1.0, 0.0).astype(BF16)}
    x = row ^ col
    lvl = jnp.full((c, c), len(_HG_LEVELS), I32)
    for li, h in enumerate(_HG_LEVELS):
        lvl = jnp.where((x >= h) & (x < 2 * h), li, lvl)
    level_id = {False: jnp.where(col <= row, lvl, -1), True: jnp.where(col >= row, lvl, -1)}

    st_ref[...] = jnp.zeros_like(st_ref)

    def chains_at(a, z, k_f, l_f, k_b, l_b, v, q):
        out = []
        for hd in range(n_heads):
            lanes = slice(hd * HG_DK, (hd + 1) * HG_DK)
            for rev, start, kk, ll in ((False, a, k_f, l_f), (True, z, k_b, l_b)):
                rows = pl.ds(start, c)
                out.append(dict(q=None if q is None else q[rows, lanes], k=kk[rows, lanes],
                                v=v[rows, lanes], g=ll[rows, lanes], reverse=rev,
                                st_ref=st_ref.at[2 * hd + int(rev)],
                                cum_ref=cum_ref.at[2 * hd + int(rev)]))
        return out

    def ctx_body(j, carry):
        a = pl.multiple_of(j * c, c)
        z = pl.multiple_of((n_ctx - 1 - j) * c, c)
        _hgrn_chunks(chains_at(a, z, ckf_ref, clf_ref, ckb_ref, clb_ref, cv_ref, None),
                     tri, level_id, False)
        return carry

    lax.fori_loop(0, n_ctx, ctx_body, 0)

    def lat_body(j, carry):
        a = pl.multiple_of(j * c, c)
        z = pl.multiple_of((n_lat - 1 - j) * c, c)
        outs = _hgrn_chunks(chains_at(a, z, kf_ref, lf_ref, kb_ref, lb_ref, v_ref, q_ref),
                            tri, level_id, True)
        for hd in range(n_heads):
            lanes = slice(hd * HG_DV, (hd + 1) * HG_DV)
            of_ref[pl.ds(a, c), lanes] = outs[2 * hd]
            ob_ref[pl.ds(z, c), lanes] = outs[2 * hd + 1]
        return carry

    lax.fori_loop(0, n_lat, lat_body, 0)

    for hd in range(n_heads):
        lanes = slice(hd * HG_DV, (hd + 1) * HG_DV)
        o = of_ref[:, lanes] + ob_ref[:, lanes]
        ms = jnp.mean(o * o, axis=-1, keepdims=True)
        o_ref[:, lanes] = (o * lax.rsqrt(ms + NORM_EPS) * gn_ref[:, lanes]).astype(BF16)


def _hgrn(hq, kf, lf, kb, lb, vi, ckf, clf, ckb, clb, cvi, gn):
    b, seq, _ = hq.shape
    n_ctx = ckf.shape[1]
    nh = HG_HEADS_PER_STEP
    lat = pl.BlockSpec((None, seq, nh * HG_DK), lambda i, h: (i, 0, h))
    cx = pl.BlockSpec((None, n_ctx, nh * HG_DK), lambda i, h: (i, 0, h))
    return pl.pallas_call(
        _hgrn_kernel,
        grid=(b, HG_HEADS // nh),
        in_specs=[lat, lat, lat, lat, lat, lat, cx, cx, cx, cx, cx,
                  pl.BlockSpec((1, nh * HG_DV), lambda i, h: (0, h))],
        out_specs=lat,
        out_shape=jax.ShapeDtypeStruct((b, seq, HG_VAL), BF16),
        scratch_shapes=[pltpu.VMEM((2 * nh, HG_DV, HG_DK), F32),
                        pltpu.VMEM((seq, nh * HG_DV), F32), pltpu.VMEM((seq, nh * HG_DV), F32),
                        pltpu.VMEM((2 * nh, HG_CHUNK, HG_DK), F32)],
        compiler_params=_params(("parallel", "parallel")),
        name="hgrn_scan",
    )(hq, kf, lf, kb, lb, vi, ckf, clf, ckb, clb, cvi, gn)


_HEADS_PER_TILE = LANES // ATT_HEAD_DIM


def _attn_kernel(q_ref, kx_ref, vx_ref, kc_ref, vc_ref, o_ref, kp_ref, vp_ref):
    kvh = pl.program_id(1)

    @pl.when(pl.program_id(2) == 0)
    def _build():
        k_all = jnp.concatenate([kc_ref[...], kx_ref[...]], axis=0)
        v_all = jnp.concatenate([vc_ref[...], vx_ref[...]], axis=0)
        r = lax.broadcasted_iota(I32, (ATT_KV, LANES), 0)
        cidx = lax.broadcasted_iota(I32, (ATT_KV, LANES), 1)
        for u in range(_HEADS_PER_TILE):
            place = (cidx // ATT_HEAD_DIM == u) & (r == kvh * ATT_HEAD_DIM + cidx - u * ATT_HEAD_DIM)
            rep = jnp.where(place, 1.0, 0.0).astype(BF16)
            kp_ref[u] = jnp.dot(k_all, rep, preferred_element_type=F32).astype(BF16)
            vp_ref[u] = jnp.dot(v_all, rep, preferred_element_type=F32).astype(BF16)

    for pair in range(ATT_GROUPS // _HEADS_PER_TILE):
        lanes = slice(pair * LANES, (pair + 1) * LANES)
        q = q_ref[:, lanes]
        acc = jnp.zeros(q.shape, F32)
        for u in range(_HEADS_PER_TILE):
            s = lax.dot_general(q, kp_ref[u], (((1,), (1,)), ((), ())), preferred_element_type=F32)
            p = jnp.exp2(s - jnp.max(s, axis=-1, keepdims=True))
            inv = 1.0 / jnp.sum(p, axis=-1, keepdims=True)
            acc = acc + jnp.dot(p.astype(BF16), vp_ref[u], preferred_element_type=F32) * inv
        o_ref[:, lanes] = acc.astype(BF16)


def _attention(aq, ak, av, cak, cav):
    b, seq, _ = aq.shape
    n_ctx = cak.shape[1]
    width = ATT_GROUPS * ATT_HEAD_DIM
    kx = pl.BlockSpec((None, seq, ATT_KV), lambda i, h, t: (i, 0, 0))
    kc = pl.BlockSpec((None, n_ctx, ATT_KV), lambda i, h, t: (i, 0, 0))
    qo = pl.BlockSpec((None, Q_TILE, width), lambda i, h, t: (i, t, h))
    return pl.pallas_call(
        _attn_kernel,
        grid=(b, ATT_KV_HEADS, seq // Q_TILE),
        in_specs=[qo, kx, kx, kc, kc],
        out_specs=qo,
        out_shape=jax.ShapeDtypeStruct((b, seq, ATT_Q), BF16),
        scratch_shapes=[pltpu.VMEM((_HEADS_PER_TILE, seq + n_ctx, LANES), BF16),
                        pltpu.VMEM((_HEADS_PER_TILE, seq + n_ctx, LANES), BF16)],
        compiler_params=_params(("parallel", "parallel", "arbitrary")),
        name="attention",
    )(aq, ak, av, cak, cav)


def _merge_kernel(on_ref, sg_ref, oa_ref, ga_ref, gb_ref, x_ref, mod_ref, n2_ref,
                  wa_ref, wb_ref, wo_ref, wrh_ref, wrl_ref, x1_ref, h2_ref, lg_ref):
    a_in = (on_ref[...].astype(F32) * sg_ref[...].astype(F32)).astype(BF16)
    a = jnp.dot(a_in, wa_ref[...], preferred_element_type=F32)
    bb = jnp.dot(oa_ref[...], wb_ref[...], preferred_element_type=F32)
    m = ga_ref[...].astype(F32) * a + gb_ref[...].astype(F32) * bb
    y = jnp.dot(m.astype(BF16), wo_ref[...], preferred_element_type=F32)
    x1 = x_ref[...] + mod_ref[2:3, :] * y
    x1_ref[...] = x1
    h2 = _rms_mod(x1, n2_ref[...], mod_ref[3:4, :], mod_ref[4:5, :])
    h_hi = h2.astype(BF16)
    h_lo = (h2 - h_hi.astype(F32)).astype(BF16)
    h2_ref[...] = h_hi
    wr_hi = wrh_ref[...]
    logits = (jnp.dot(h_hi, wr_hi, preferred_element_type=F32)
              + jnp.dot(h_lo, wr_hi, preferred_element_type=F32)
              + jnp.dot(h_hi, wrl_ref[...], preferred_element_type=F32))
    lg_ref[...] = logits[:, :N_EXPERTS]


def _merge(on, sg, oatt, sga, sgb, x, mods, norm2_g, wa, wb, wo, wr_hi, wr_lo):
    b, seq, d = x.shape
    tm = TOKEN_TILE
    tok = lambda n: pl.BlockSpec((None, tm, n), lambda i, t: (i, t, 0))
    return pl.pallas_call(
        _merge_kernel,
        grid=(b, seq // tm),
        in_specs=[tok(HG_VAL), tok(HG_VAL), tok(ATT_Q), tok(d), tok(d), tok(d),
                  pl.BlockSpec((None, N_MOD, d), lambda i, t: (i, 0, 0)),
                  _const_spec((1, d)), _const_spec(wa.shape), _const_spec(wb.shape),
                  _const_spec(wo.shape), _const_spec(wr_hi.shape), _const_spec(wr_lo.shape)],
        out_specs=[tok(d), tok(d), tok(N_EXPERTS)],
        out_shape=[jax.ShapeDtypeStruct((b, seq, d), F32),
                   jax.ShapeDtypeStruct((b, seq, d), BF16),
                   jax.ShapeDtypeStruct((b, seq, N_EXPERTS), F32)],
        compiler_params=_params(("parallel", "parallel")),
        name="merge",
    )(on, sg, oatt, sga, sgb, x, mods, norm2_g, wa, wb, wo, wr_hi, wr_lo)


def _route_kernel(lg_ref, pos_ref, gate_ref, *, cap):
    lg = lg_ref[...]
    n_e, length = lg.shape
    e = jnp.exp(lg - jnp.max(lg, axis=0, keepdims=True))
    aff = e / jnp.sum(e, axis=0, keepdims=True)
    bits = lax.bitcast_convert_type(aff, I32)

    def count(mask):
        return jnp.sum(jnp.where(mask, 1.0, 0.0), axis=1, keepdims=True)

    def value_step(i, t):
        cand = t | lax.shift_left(jnp.int32(1), (30 - i).astype(I32))
        return jnp.where(count(bits >= cand) >= cap, cand, t)

    thr = lax.fori_loop(0, 31, value_step, jnp.zeros((n_e, 1), I32))
    above = bits > thr
    tied = bits == thr
    need = cap - count(above)
    idx = lax.broadcasted_iota(I32, (1, length), 1)
    n_bits = int(np.log2(length))

    def index_step(i, j):
        cand = j | lax.shift_left(jnp.int32(1), (n_bits - 1 - i).astype(I32))
        return jnp.where(count(tied & (idx < cand)) < need, cand, j)

    last = lax.fori_loop(0, n_bits, index_step, jnp.zeros((n_e, 1), I32))
    sel = above | (tied & (idx <= last))

    r = lax.broadcasted_iota(I32, (LANES, LANES), 0)
    cc = lax.broadcasted_iota(I32, (LANES, LANES), 1)
    before = jnp.where(r < cc, 1.0, 0.0).astype(BF16)
    sel_b = jnp.where(sel, 1.0, 0.0).astype(BF16)
    offset = jnp.zeros((n_e, 1), F32)
    pieces = []
    for gidx in range(length // LANES):
        blk = sel_b[:, gidx * LANES:(gidx + 1) * LANES]
        pieces.append(jnp.dot(blk, before, preferred_element_type=F32) + offset)
        offset = offset + jnp.sum(blk.astype(F32), axis=1, keepdims=True)
    pos = jnp.concatenate(pieces, axis=1)
    pos_ref[...] = jnp.where(sel, pos.astype(I32), -1)
    gate_ref[...] = jnp.where(sel, aff, 0.0)


def _route(logits_t, cap):
    b, n_e, length = logits_t.shape
    spec = pl.BlockSpec((None, n_e, length), lambda i: (i, 0, 0))
    return pl.pallas_call(
        functools.partial(_route_kernel, cap=cap),
        grid=(b,),
        in_specs=[spec],
        out_specs=[spec, spec],
        out_shape=[jax.ShapeDtypeStruct((b, n_e, length), I32),
                   jax.ShapeDtypeStruct((b, n_e, length), F32)],
        compiler_params=_params(("parallel",)),
        name="route",
    )(logits_t)


def _gather_kernel(pos_ref, h_ref, xe_ref):
    ge, cap, d = xe_ref.shape
    length = h_ref.shape[0]
    slot = lax.broadcasted_iota(I32, (cap, length), 0)
    base = pl.program_id(1) * ge
    pieces = [jnp.where(pos_ref[pl.ds(base + e, 1), :] == slot, 1.0, 0.0).astype(BF16)
              for e in range(ge)]
    onehot = jnp.concatenate(pieces, axis=0)
    xe = jnp.dot(onehot, h_ref[...], preferred_element_type=F32)
    xe_ref[...] = xe.astype(BF16).reshape(ge, cap, d)


def _gather(pos, h2, cap):
    b, n_e, length = pos.shape
    d = h2.shape[-1]
    ge = GATHER_EXPERT_GROUP
    return pl.pallas_call(
        _gather_kernel,
        grid=(b, n_e // ge),
        in_specs=[pl.BlockSpec((None, n_e, length), lambda i, e: (i, 0, 0)),
                  pl.BlockSpec((None, length, d), lambda i, e: (i, 0, 0))],
        out_specs=pl.BlockSpec((None, ge, cap, d), lambda i, e: (i, e, 0, 0)),
        out_shape=jax.ShapeDtypeStruct((b, n_e, cap, d), BF16),
        compiler_params=_params(("parallel", "arbitrary")),
        name="gather",
    )(pos, h2)


def _ffn_kernel(xe_ref, wg_ref, wu_ref, wd_ref, ye_ref):
    nb, cap, d = xe_ref.shape
    x = xe_ref[...].reshape(nb * cap, d)
    hg = jnp.dot(x, wg_ref[...], preferred_element_type=F32)
    hu = jnp.dot(x, wu_ref[...], preferred_element_type=F32)
    hid = (hg * jax.nn.sigmoid(hg) * hu).astype(BF16)
    ye = jnp.dot(hid, wd_ref[...], preferred_element_type=F32)
    ye_ref[...] = ye.astype(BF16).reshape(nb, cap, d)


def _ffn(xe, wg, wu, wd):
    b, n_e, cap, d = xe.shape
    ff = wg.shape[-1]
    nb = int(np.gcd(b, FFN_BATCH_GROUP))
    tok = pl.BlockSpec((nb, None, cap, d), lambda e, i: (i, e, 0, 0))
    return pl.pallas_call(
        _ffn_kernel,
        grid=(n_e, b // nb),
        in_specs=[tok,
                  pl.BlockSpec((None, d, ff), lambda e, i: (e, 0, 0)),
                  pl.BlockSpec((None, d, ff), lambda e, i: (e, 0, 0)),
                  pl.BlockSpec((None, ff, d), lambda e, i: (e, 0, 0))],
        out_specs=tok,
        out_shape=jax.ShapeDtypeStruct((b, n_e, cap, d), BF16),
        compiler_params=_params(("parallel", "arbitrary")),
        name="expert_ffn",
    )(xe, wg, wu, wd)


def _combine_kernel(pos_ref, gate_ref, ye_ref, x1_ref, mod_ref, fg_ref, o_ref):
    n_e, cap, d = ye_ref.shape
    tm = x1_ref.shape[0]
    pos = pos_ref[...]
    gate = gate_ref[...]
    slot = lax.broadcasted_iota(I32, (tm, cap), 1)
    pieces = [jnp.where(pos[:, e:e + 1] == slot, gate[:, e:e + 1], 0.0).astype(BF16)
              for e in range(n_e)]
    scatter = jnp.concatenate(pieces, axis=1)
    y = jnp.dot(scatter, ye_ref[...].reshape(n_e * cap, d), preferred_element_type=F32)
    x2 = x1_ref[...] + mod_ref[5:6, :] * y
    ms = jnp.mean(x2 * x2, axis=-1, keepdims=True)
    o_ref[...] = x2 * lax.rsqrt(ms + NORM_EPS) * fg_ref[...]


def _combine(pos_t, gate_t, ye, x1, mods, final_g):
    b, seq, d = x1.shape
    n_e, cap = ye.shape[1], ye.shape[2]
    tm = COMBINE_TILE
    tok = lambda n: pl.BlockSpec((None, tm, n), lambda i, t: (i, t, 0))
    return pl.pallas_call(
        _combine_kernel,
        grid=(b, seq // tm),
        in_specs=[tok(n_e), tok(n_e),
                  pl.BlockSpec((None, n_e, cap, d), lambda i, t: (i, 0, 0, 0)),
                  tok(d),
                  pl.BlockSpec((None, N_MOD, d), lambda i, t: (i, 0, 0)),
                  _const_spec((1, d))],
        out_specs=tok(d),
        out_shape=jax.ShapeDtypeStruct((b, seq, d), F32),
        compiler_params=_params(("parallel", "arbitrary")),
        name="combine",
    )(pos_t, gate_t, ye, x1, mods, final_g)


def _rope_tables(length):
    rows = length // GRID_W
    row = jnp.repeat(jnp.arange(rows, dtype=F32), GRID_W)
    col = jnp.tile(jnp.arange(GRID_W, dtype=F32), rows)
    inv_freq = ROPE_THETA ** (-jnp.arange(0, ROPE_AXIS_DIM, 2, dtype=F32) / ROPE_AXIS_DIM)
    half = ROPE_AXIS_DIM // 2
    cos_parts, sin_parts = [], []
    for pos in (row, col):
        ang = pos[:, None] * inv_freq
        cos_parts += [jnp.cos(ang), jnp.cos(ang)]
        sin_parts += [-jnp.sin(ang), jnp.sin(ang)]
    cos = jnp.concatenate(cos_parts, axis=-1)
    sin = jnp.concatenate(sin_parts, axis=-1)
    reps = LANES // ATT_HEAD_DIM
    assert half * 4 == ATT_HEAD_DIM
    return jnp.tile(cos, (1, reps)), jnp.tile(sin, (1, reps))


def kernel(x, c, ctx, c_ctx, w_mod, b_mod, norm1_g, norm2_g, w_in, hg_lb_logits, hg_norm_g,
           q_norm_g, k_norm_g, w_branch_a, w_branch_b, w_out, w_router, w_exp_gate, w_exp_up,
           w_exp_down, final_norm_g):
    b, seq, d = x.shape
    depth = w_mod.shape[0]
    assert depth == 1, "context-stream update between layers is not implemented"
    layer = 0
    cap = EC_CAPACITY_FACTOR * seq // N_EXPERTS

    n_rows = -(-(b + 1) // SUBLANES) * SUBLANES
    cond = jnp.zeros((n_rows, d), F32).at[:b].set(c).at[b].set(c_ctx)
    mods = _adaln(cond, w_mod[layer], b_mod[layer]).reshape(n_rows, N_MOD, d)

    cos, sin = _rope_tables(seq)
    gsum = jnp.asarray(np.kron(np.eye(ATT_HEADS), np.ones((ATT_HEAD_DIM, ATT_HEAD_DIM))), BF16)
    qg = jnp.tile(q_norm_g[layer], ATT_HEADS).reshape(1, ATT_Q)
    kg = jnp.tile(k_norm_g[layer], ATT_KV_HEADS).reshape(1, ATT_KV)
    w_bf = w_in[layer].astype(BF16)
    w_ctx = jnp.concatenate([w_bf[:, _OFF_FF:_OFF_G], w_bf[:, _OFF_AK:_OFF_AV + ATT_KV]], axis=1)
    n1 = norm1_g[layer].reshape(1, d)

    hq, kf, lf, kb, lb, vi, sg, aq, ak, av, sga, sgb = _inproj_latent(
        x, mods, n1, w_bf, hg_lb_logits, qg, kg, cos, sin, gsum, layer)
    ckf, clf, ckb, clb, cvi, cak, cav = _inproj_context(
        ctx, mods, b, n1, w_ctx, hg_lb_logits, kg, gsum[:ATT_KV, :ATT_KV], layer)

    on = _hgrn(hq, kf, lf, kb, lb, vi, ckf, clf, ckb, clb, cvi,
               hg_norm_g[layer].reshape(1, HG_VAL))
    oatt = _attention(aq, ak, av, cak, cav)

    wr = jnp.zeros((d, LANES), F32).at[:, :N_EXPERTS].set(w_router[layer])
    wr_hi = wr.astype(BF16)
    wr_lo = (wr - wr_hi.astype(F32)).astype(BF16)
    x1, h2, logits = _merge(on, sg, oatt, sga, sgb, x, mods, norm2_g[layer].reshape(1, d),
                            w_branch_a[layer].astype(BF16), w_branch_b[layer].astype(BF16),
                            w_out[layer].astype(BF16), wr_hi, wr_lo)

    pos, gate = _route(jnp.swapaxes(logits, 1, 2), cap)
    xe = _gather(pos, h2, cap)
    ye = _ffn(xe, w_exp_gate[layer].astype(BF16), w_exp_up[layer].astype(BF16),
              w_exp_down[layer].astype(BF16))
    return _combine(jnp.swapaxes(pos, 1, 2), jnp.swapaxes(gate, 1, 2), ye, x1, mods,
                    final_norm_g.reshape(1, d))
```

```python
import functools

import numpy as np
import jax
import jax.numpy as jnp
from jax import lax
from jax.experimental import pallas as pl
from jax.experimental.pallas import tpu as pltpu

F32 = jnp.float32
BF16 = jnp.bfloat16
I32 = jnp.int32

GRID_W = 64
HG_HEADS = 4
HG_DK = 128
HG_DV = 128
HG_KEY = HG_HEADS * HG_DK
HG_VAL = HG_HEADS * HG_DV
ATT_HEADS = 8
ATT_KV_HEADS = 2
ATT_HEAD_DIM = 64
ATT_GROUPS = ATT_HEADS // ATT_KV_HEADS
ATT_Q = ATT_HEADS * ATT_HEAD_DIM
ATT_KV = ATT_KV_HEADS * ATT_HEAD_DIM
ROPE_AXIS_DIM = ATT_HEAD_DIM // 2
ROPE_THETA = 10000.0
N_EXPERTS = 16
EC_CAPACITY_FACTOR = 2
N_MOD = 6
NORM_EPS = 1e-6
LOG2_E = float(np.log2(np.e))

LANES = 128
SUBLANES = 8
VMEM_LIMIT_BYTES = 56 * 1024 * 1024

HG_CHUNK = 64
HG_HEADS_PER_STEP = 4
TOKEN_TILE = 512
CTX_TILE = 256
Q_TILE = 512
Q_TILES_PER_STEP = 2
FFN_BATCH_GROUP = 4
GATHER_EXPERT_GROUP = 4
COMBINE_TILE = 1024

_OFF_HQ = 0
_OFF_FF = _OFF_HQ + HG_KEY
_OFF_FB = _OFF_FF + HG_KEY
_OFF_I = _OFF_FB + HG_KEY
_OFF_G = _OFF_I + HG_VAL
_OFF_AQ = _OFF_G + HG_VAL
_OFF_AK = _OFF_AQ + ATT_Q
_OFF_AV = _OFF_AK + ATT_KV


def _params(sem):
    return pltpu.CompilerParams(dimension_semantics=sem, vmem_limit_bytes=VMEM_LIMIT_BYTES)


def _const_spec(shape):
    zeros = (0,) * len(shape)
    return pl.BlockSpec(shape, lambda *_: zeros)


def _adaln_kernel(c_ref, w_ref, b_ref, o_ref):
    c = c_ref[...]
    a = c * jax.nn.sigmoid(c)
    o_ref[...] = jnp.dot(a, w_ref[...], preferred_element_type=F32,
                         precision=lax.Precision.HIGHEST) + b_ref[...]


def _adaln(cond, w_mod, b_mod):
    n, d = cond.shape
    ncol = w_mod.shape[1]
    return pl.pallas_call(
        _adaln_kernel,
        grid=(ncol // d,),
        in_specs=[pl.BlockSpec((n, d), lambda j: (0, 0)),
                  pl.BlockSpec((d, d), lambda j: (0, j)),
                  pl.BlockSpec((1, d), lambda j: (0, j))],
        out_specs=pl.BlockSpec((n, d), lambda j: (0, j)),
        out_shape=jax.ShapeDtypeStruct((n, ncol), F32),
        compiler_params=_params(("arbitrary",)),
        name="adaln",
    )(cond, w_mod, b_mod.reshape(1, ncol))


def _rms_mod(x, gain, shift, scale):
    ms = jnp.mean(x * x, axis=-1, keepdims=True)
    h = x * lax.rsqrt(ms + NORM_EPS) * gain
    return h * (1.0 + scale) + shift


def _lower_bound(lbl_ref, direction, layer):
    lg = lbl_ref[direction]
    e = jnp.exp(lg - jnp.max(lg, axis=0, keepdims=True))
    p = e / jnp.sum(e, axis=0, keepdims=True)
    return jnp.sum(p[:layer + 1], axis=0, keepdims=True)


def _hgrn_gate(f_pre, lb):
    s = jax.nn.sigmoid(f_pre)
    return (1.0 - lb) * (1.0 - s), jnp.log2(lb + (1.0 - lb) * s)


def _head_rms(a, gsum, gain):
    ssq = jnp.dot((a * a).astype(BF16), gsum, preferred_element_type=F32)
    return a * lax.rsqrt(ssq * (1.0 / ATT_HEAD_DIM) + NORM_EPS) * gain


def _rope(a, cos, sin):
    lane = lax.broadcasted_iota(I32, (1, LANES), 1)
    first = (lane % ROPE_AXIS_DIM) < (ROPE_AXIS_DIM // 2)
    half = ROPE_AXIS_DIM // 2
    outs = []
    for j in range(a.shape[1] // LANES):
        xg = a[:, j * LANES:(j + 1) * LANES]
        partner = jnp.where(first, pltpu.roll(xg, LANES - half, 1), pltpu.roll(xg, half, 1))
        outs.append(xg * cos + partner * sin)
    return outs[0] if len(outs) == 1 else jnp.concatenate(outs, axis=1)


def _inproj_latent_kernel(x_ref, mod_ref, g_ref, w_ref, lbl_ref, qg_ref, kg_ref, cos_ref, sin_ref,
                          gs_ref, hq_ref, kf_ref, lf_ref, kb_ref, lb_ref, vi_ref, sg_ref,
                          aq_ref, ak_ref, av_ref, ga_ref, gb_ref, *, layer):
    hb = _rms_mod(x_ref[...], g_ref[...], mod_ref[0:1, :], mod_ref[1:2, :]).astype(BF16)

    def proj(lo, n):
        return jnp.dot(hb, w_ref[:, lo:lo + n], preferred_element_type=F32)

    hq_ref[...] = proj(_OFF_HQ, HG_KEY).astype(BF16)
    key, logf = _hgrn_gate(proj(_OFF_FF, HG_KEY), _lower_bound(lbl_ref, 0, layer))
    kf_ref[...] = key.astype(BF16)
    lf_ref[...] = logf
    key, logf = _hgrn_gate(proj(_OFF_FB, HG_KEY), _lower_bound(lbl_ref, 1, layer))
    kb_ref[...] = key.astype(BF16)
    lb_ref[...] = logf
    vi_ref[...] = proj(_OFF_I, HG_VAL).astype(BF16)
    g = proj(_OFF_G, HG_VAL)
    sg_ref[...] = (g * jax.nn.sigmoid(g)).astype(BF16)
    cos = cos_ref[...]
    sin = sin_ref[...]
    aq = _head_rms(proj(_OFF_AQ, ATT_Q), gs_ref[...], qg_ref[...])
    aq_ref[...] = (_rope(aq, cos, sin) * (ATT_HEAD_DIM ** -0.5 * LOG2_E)).astype(BF16)
    ak = _head_rms(proj(_OFF_AK, ATT_KV), gs_ref[0:ATT_KV, 0:ATT_KV], kg_ref[...])
    ak_ref[...] = _rope(ak, cos, sin).astype(BF16)
    av_ref[...] = proj(_OFF_AV, ATT_KV).astype(BF16)
    d = ga_ref.shape[-1]
    ga_ref[...] = jax.nn.sigmoid(proj(_OFF_AV + ATT_KV, d)).astype(BF16)
    gb_ref[...] = jax.nn.sigmoid(proj(_OFF_AV + ATT_KV + d, d)).astype(BF16)


def _inproj_context_kernel(x_ref, mod_ref, g_ref, w_ref, lbl_ref, kg_ref, gs_ref,
                           kf_ref, lf_ref, kb_ref, lb_ref, vi_ref, ak_ref, av_ref, *, layer):
    hb = _rms_mod(x_ref[...], g_ref[...], mod_ref[0:1, :], mod_ref[1:2, :]).astype(BF16)

    def proj(lo, n):
        return jnp.dot(hb, w_ref[:, lo:lo + n], preferred_element_type=F32)

    key, logf = _hgrn_gate(proj(0, HG_KEY), _lower_bound(lbl_ref, 0, layer))
    kf_ref[...] = key.astype(BF16)
    lf_ref[...] = logf
    key, logf = _hgrn_gate(proj(HG_KEY, HG_KEY), _lower_bound(lbl_ref, 1, layer))
    kb_ref[...] = key.astype(BF16)
    lb_ref[...] = logf
    vi_ref[...] = proj(2 * HG_KEY, HG_VAL).astype(BF16)
    ak = _head_rms(proj(2 * HG_KEY + HG_VAL, ATT_KV), gs_ref[...], kg_ref[...])
    ak_ref[...] = ak.astype(BF16)
    av_ref[...] = proj(2 * HG_KEY + HG_VAL + ATT_KV, ATT_KV).astype(BF16)


def _inproj_latent(x, mods, norm_g, w_bf, lbl, qg, kg, cos, sin, gsum, layer):
    b, seq, d = x.shape
    tm = TOKEN_TILE
    tok = lambda n: pl.BlockSpec((None, tm, n), lambda i, t: (i, t, 0))
    shp = lambda n, dt: jax.ShapeDtypeStruct((b, seq, n), dt)
    return pl.pallas_call(
        functools.partial(_inproj_latent_kernel, layer=layer),
        grid=(b, seq // tm),
        in_specs=[tok(d),
                  pl.BlockSpec((None, N_MOD, d), lambda i, t: (i, 0, 0)),
                  _const_spec((1, d)),
                  _const_spec(w_bf.shape),
                  _const_spec(lbl.shape),
                  _const_spec(qg.shape),
                  _const_spec(kg.shape),
                  pl.BlockSpec((tm, LANES), lambda i, t: (t, 0)),
                  pl.BlockSpec((tm, LANES), lambda i, t: (t, 0)),
                  _const_spec(gsum.shape)],
        out_specs=[tok(HG_KEY), tok(HG_KEY), tok(HG_KEY), tok(HG_KEY), tok(HG_KEY), tok(HG_VAL),
                   tok(HG_VAL), tok(ATT_Q), tok(ATT_KV), tok(ATT_KV), tok(d), tok(d)],
        out_shape=[shp(HG_KEY, BF16), shp(HG_KEY, BF16), shp(HG_KEY, F32), shp(HG_KEY, BF16),
                   shp(HG_KEY, F32), shp(HG_VAL, BF16), shp(HG_VAL, BF16), shp(ATT_Q, BF16),
                   shp(ATT_KV, BF16), shp(ATT_KV, BF16), shp(d, BF16), shp(d, BF16)],
        compiler_params=_params(("parallel", "parallel")),
        name="inproj_latent",
    )(x, mods, norm_g, w_bf, lbl, qg, kg, cos, sin, gsum)


def _inproj_context(ctx, mods, ctx_row, norm_g, w_bf, lbl, kg, gsum, layer):
    b, n_ctx, d = ctx.shape
    tm = CTX_TILE
    tok = lambda n: pl.BlockSpec((None, tm, n), lambda i, t: (i, t, 0))
    shp = lambda n, dt: jax.ShapeDtypeStruct((b, n_ctx, n), dt)
    return pl.pallas_call(
        functools.partial(_inproj_context_kernel, layer=layer),
        grid=(b, n_ctx // tm),
        in_specs=[tok(d),
                  pl.BlockSpec((None, N_MOD, d), lambda i, t: (ctx_row, 0, 0)),
                  _const_spec((1, d)),
                  _const_spec(w_bf.shape),
                  _const_spec(lbl.shape),
                  _const_spec(kg.shape),
                  _const_spec(gsum.shape)],
        out_specs=[tok(HG_KEY), tok(HG_KEY), tok(HG_KEY), tok(HG_KEY), tok(HG_VAL),
                   tok(ATT_KV), tok(ATT_KV)],
        out_shape=[shp(HG_KEY, BF16), shp(HG_KEY, F32), shp(HG_KEY, BF16), shp(HG_KEY, F32),
                   shp(HG_VAL, BF16), shp(ATT_KV, BF16), shp(ATT_KV, BF16)],
        compiler_params=_params(("parallel", "parallel")),
        name="inproj_context",
    )(ctx, mods, norm_g, w_bf, lbl, kg, gsum)


_HG_LEVELS = tuple(2 ** i for i in range(int(np.log2(HG_CHUNK))))


def _level_operand(h, ch):
    c = HG_CHUNK
    g, cum, cum_ref, reverse = ch["g"], ch["cum"], ch["cum_ref"], ch["reverse"]
    q32, k32 = ch["q32"], ch["k32"]
    if h >= SUBLANES:
        src, arg = [], []
        for p in range(c // (2 * h)):
            first = slice(p * 2 * h, p * 2 * h + h)
            second = slice(p * 2 * h + h, (p + 1) * 2 * h)
            r = p * 2 * h + (h if reverse else h - 1)
            mid = cum_ref[r:r + 1, :]
            if reverse:
                src += [q32[first], k32[second]]
                arg += [cum[first] - mid, mid - cum[second]]
            else:
                src += [k32[first], q32[second]]
                arg += [mid - cum[first], cum[second] - mid]
        return (jnp.concatenate(src, axis=0) * jnp.exp2(jnp.concatenate(arg, axis=0))).astype(BF16)
    row = lax.broadcasted_iota(I32, (c, 1), 0)
    u = row % (2 * h)
    query = (u < h) if reverse else (u >= h)
    src = jnp.where(query, q32, k32)
    if h == 1:
        arg = jnp.where(query, g, 0.0)
    elif h == 2:
        up = pltpu.roll(g, c - 1, 0)
        down = pltpu.roll(g, 1, 0)
        if reverse:
            arg = jnp.where(u == 0, g + up, jnp.where(u == 1, g, jnp.where(u == 2, 0.0, down)))
        else:
            arg = jnp.where(u == 0, up, jnp.where(u == 1, 0.0, jnp.where(u == 2, g, g + down)))
    else:
        pieces = []
        for p in range(c // (2 * h)):
            r = p * 2 * h + (h if reverse else h - 1)
            pieces.append(jnp.broadcast_to(cum_ref[r:r + 1, :], (2 * h, HG_DK)))
        arg = -jnp.abs(cum - jnp.concatenate(pieces, axis=0))
    return (src * jnp.exp2(arg)).astype(BF16)


def _hgrn_chunks(chains, tri, level_id, want_o):
    c = HG_CHUNK
    nt = (((1,), (1,)), ((), ()))
    tn = (((0,), (0,)), ((), ()))
    for ch in chains:
        g = ch["g"]
        g_hi = g.astype(BF16)
        g_lo = (g - g_hi.astype(F32)).astype(BF16)
        t = tri[ch["reverse"]]
        ch["cum"] = (jnp.dot(t, g_hi, preferred_element_type=F32)
                     + jnp.dot(t, g_lo, preferred_element_type=F32))
    for ch in chains:
        cum = ch["cum"]
        last = cum[0:1, :] if ch["reverse"] else cum[c - 1:c, :]
        ch["k32"] = ch["k"].astype(F32)
        kl = (ch["k32"] * jnp.exp2(last - cum)).astype(BF16)
        ch["st"] = ch["st_ref"][...]
        ch["st_ref"][...] = ch["st"] * jnp.exp2(last) + lax.dot_general(
            ch["v"], kl, tn, preferred_element_type=F32)
    if not want_o:
        return None
    for ch in chains:
        ch["q32"] = ch["q"].astype(F32)
        ch["cum_ref"][...] = ch["cum"]
        ch["scores"] = jnp.zeros((c, c), F32)
    for li, h in enumerate(_HG_LEVELS):
        for ch in chains:
            m = _level_operand(h, ch)
            s_h = lax.dot_general(m, m, nt, preferred_element_type=F32)
            ch["scores"] = jnp.where(level_id[ch["reverse"]] == li, s_h, ch["scores"])
    outs = []
    for ch in chains:
        self_term = jnp.sum(ch["q32"] * ch["k32"], axis=-1, keepdims=True)
        scores = jnp.where(level_id[ch["reverse"]] == len(_HG_LEVELS), self_term, ch["scores"])
        o = jnp.dot(scores.astype(BF16), ch["v"], preferred_element_type=F32)
        o = o + lax.dot_general((ch["q32"] * jnp.exp2(ch["cum"])).astype(BF16),
                                ch["st"].astype(BF16), nt, preferred_element_type=F32)
        outs.append(o)
    return outs


def _hgrn_kernel(q_ref, kf_ref, lf_ref, kb_ref, lb_ref, v_ref,
                 ckf_ref, clf_ref, ckb_ref, clb_ref, cv_ref, gn_ref, o_ref,
                 st_ref, of_ref, ob_ref, cum_ref):
    c = HG_CHUNK
    n_heads = q_ref.shape[1] // HG_DK
    n_ctx = ckf_ref.shape[0] // c
    n_lat = q_ref.shape[0] // c
    row = lax.broadcasted_iota(I32, (c, c), 0)
    col = lax.broadcasted_iota(I32, (c, c), 1)
    tri = {False: jnp.where(col <= row, 1.0, 0.0).astype(BF16),
           True: jnp.where(col >= row, 1.0, 0.0).astype(BF16)}
    x = row ^ col
    lvl = jnp.full((c, c), len(_HG_LEVELS), I32)
    for li, h in enumerate(_HG_LEVELS):
        lvl = jnp.where((x >= h) & (x < 2 * h), li, lvl)
    level_id = {False: jnp.where(col <= row, lvl, -1), True: jnp.where(col >= row, lvl, -1)}

    st_ref[...] = jnp.zeros_like(st_ref)

    def chains_at(a, z, k_f, l_f, k_b, l_b, v, q):
        out = []
        for hd in range(n_heads):
            lanes = slice(hd * HG_DK, (hd + 1) * HG_DK)
            for rev, start, kk, ll in ((False, a, k_f, l_f), (True, z, k_b, l_b)):
                rows = pl.ds(start, c)
                out.append(dict(q=None if q is None else q[rows, lanes], k=kk[rows, lanes],
                                v=v[rows, lanes], g=ll[rows, lanes], reverse=rev,
                                st_ref=st_ref.at[2 * hd + int(rev)],
                                cum_ref=cum_ref.at[2 * hd + int(rev)]))
        return out

    def ctx_body(j, carry):
        a = pl.multiple_of(j * c, c)
        z = pl.multiple_of((n_ctx - 1 - j) * c, c)
        _hgrn_chunks(chains_at(a, z, ckf_ref, clf_ref, ckb_ref, clb_ref, cv_ref, None),
                     tri, level_id, False)
        return carry

    lax.fori_loop(0, n_ctx, ctx_body, 0)

    def lat_body(j, carry):
        a = pl.multiple_of(j * c, c)
        z = pl.multiple_of((n_lat - 1 - j) * c, c)
        outs = _hgrn_chunks(chains_at(a, z, kf_ref, lf_ref, kb_ref, lb_ref, v_ref, q_ref),
                            tri, level_id, True)
        for hd in range(n_heads):
            lanes = slice(hd * HG_DV, (hd + 1) * HG_DV)
            of_ref[pl.ds(a, c), lanes] = outs[2 * hd]
            ob_ref[pl.ds(z, c), lanes] = outs[2 * hd + 1]
        return carry

    lax.fori_loop(0, n_lat, lat_body, 0)

    for hd in range(n_heads):
        lanes = slice(hd * HG_DV, (hd + 1) * HG_DV)
        o = of_ref[:, lanes] + ob_ref[:, lanes]
        ms = jnp.mean(o * o, axis=-1, keepdims=True)
        o_ref[:, lanes] = (o * lax.rsqrt(ms + NORM_EPS) * gn_ref[:, lanes]).astype(BF16)


def _hgrn(hq, kf, lf, kb, lb, vi, ckf, clf, ckb, clb, cvi, gn):
    b, seq, _ = hq.shape
    n_ctx = ckf.shape[1]
    nh = HG_HEADS_PER_STEP
    lat = pl.BlockSpec((None, seq, nh * HG_DK), lambda i, h: (i, 0, h))
    cx = pl.BlockSpec((None, n_ctx, nh * HG_DK), lambda i, h: (i, 0, h))
    return pl.pallas_call(
        _hgrn_kernel,
        grid=(b, HG_HEADS // nh),
        in_specs=[lat, lat, lat, lat, lat, lat, cx, cx, cx, cx, cx,
                  pl.BlockSpec((1, nh * HG_DV), lambda i, h: (0, h))],
        out_specs=lat,
        out_shape=jax.ShapeDtypeStruct((b, seq, HG_VAL), BF16),
        scratch_shapes=[pltpu.VMEM((2 * nh, HG_DV, HG_DK), F32),
                        pltpu.VMEM((seq, nh * HG_DV), F32), pltpu.VMEM((seq, nh * HG_DV), F32),
                        pltpu.VMEM((2 * nh, HG_CHUNK, HG_DK), F32)],
        compiler_params=_params(("parallel", "parallel")),
        name="hgrn_scan",
    )(hq, kf, lf, kb, lb, vi, ckf, clf, ckb, clb, cvi, gn)


_HEADS_PER_TILE = LANES // ATT_HEAD_DIM


def _attn_kernel(q_ref, kx_ref, vx_ref, kc_ref, vc_ref, o_ref, kp_ref, vp_ref):
    kvh = pl.program_id(1)

    @pl.when(pl.program_id(2) == 0)
    def _build():
        k_all = jnp.concatenate([kc_ref[...], kx_ref[...]], axis=0)
        v_all = jnp.concatenate([vc_ref[...], vx_ref[...]], axis=0)
        r = lax.broadcasted_iota(I32, (ATT_KV, LANES), 0)
        cidx = lax.broadcasted_iota(I32, (ATT_KV, LANES), 1)
        for u in range(_HEADS_PER_TILE):
            place = (cidx // ATT_HEAD_DIM == u) & (r == kvh * ATT_HEAD_DIM + cidx - u * ATT_HEAD_DIM)
            rep = jnp.where(place, 1.0, 0.0).astype(BF16)
            kp_ref[u] = jnp.dot(k_all, rep, preferred_element_type=F32).astype(BF16)
            vp_ref[u] = jnp.dot(v_all, rep, preferred_element_type=F32).astype(BF16)

    for t in range(q_ref.shape[0] // Q_TILE):
        rows = slice(t * Q_TILE, (t + 1) * Q_TILE)
        for pair in range(ATT_GROUPS // _HEADS_PER_TILE):
            lanes = slice(pair * LANES, (pair + 1) * LANES)
            q = q_ref[rows, lanes]
            acc = jnp.zeros(q.shape, F32)
            for u in range(_HEADS_PER_TILE):
                s = lax.dot_general(q, kp_ref[u], (((1,), (1,)), ((), ())),
                                    preferred_element_type=F32)
                p = jnp.exp2(s - jnp.max(s, axis=-1, keepdims=True))
                inv = 1.0 / jnp.sum(p, axis=-1, keepdims=True)
                acc = acc + jnp.dot(p.astype(BF16), vp_ref[u], preferred_element_type=F32) * inv
            o_ref[rows, lanes] = acc.astype(BF16)


def _attention(aq, ak, av, cak, cav):
    b, seq, _ = aq.shape
    n_ctx = cak.shape[1]
    width = ATT_GROUPS * ATT_HEAD_DIM
    kx = pl.BlockSpec((None, seq, ATT_KV), lambda i, h, t: (i, 0, 0))
    kc = pl.BlockSpec((None, n_ctx, ATT_KV), lambda i, h, t: (i, 0, 0))
    rows = Q_TILE * Q_TILES_PER_STEP
    qo = pl.BlockSpec((None, rows, width), lambda i, h, t: (i, t, h))
    return pl.pallas_call(
        _attn_kernel,
        grid=(b, ATT_KV_HEADS, seq // rows),
        in_specs=[qo, kx, kx, kc, kc],
        out_specs=qo,
        out_shape=jax.ShapeDtypeStruct((b, seq, ATT_Q), BF16),
        scratch_shapes=[pltpu.VMEM((_HEADS_PER_TILE, seq + n_ctx, LANES), BF16),
                        pltpu.VMEM((_HEADS_PER_TILE, seq + n_ctx, LANES), BF16)],
        compiler_params=_params(("parallel", "parallel", "arbitrary")),
        name="attention",
    )(aq, ak, av, cak, cav)


def _merge_kernel(on_ref, sg_ref, oa_ref, ga_ref, gb_ref, x_ref, mod_ref, n2_ref,
                  wa_ref, wb_ref, wo_ref, wr_ref, x1_ref, h2_ref, lg_ref):
    a_in = (on_ref[...].astype(F32) * sg_ref[...].astype(F32)).astype(BF16)
    a = jnp.dot(a_in, wa_ref[...], preferred_element_type=F32)
    bb = jnp.dot(oa_ref[...], wb_ref[...], preferred_element_type=F32)
    m = ga_ref[...].astype(F32) * a + gb_ref[...].astype(F32) * bb
    y = jnp.dot(m.astype(BF16), wo_ref[...], preferred_element_type=F32)
    x1 = x_ref[...] + mod_ref[2:3, :] * y
    x1_ref[...] = x1
    h2 = _rms_mod(x1, n2_ref[...], mod_ref[3:4, :], mod_ref[4:5, :])
    h_hi = h2.astype(BF16)
    h_lo = (h2 - h_hi.astype(F32)).astype(BF16)
    h2_ref[...] = h_hi
    wr = wr_ref[...]
    r = (jnp.dot(h_hi, wr, preferred_element_type=F32)
         + jnp.dot(h_lo, wr, preferred_element_type=F32))
    lg_ref[...] = r[:, :N_EXPERTS] + r[:, N_EXPERTS:2 * N_EXPERTS]


def _merge(on, sg, oatt, sga, sgb, x, mods, norm2_g, wa, wb, wo, wr):
    b, seq, d = x.shape
    tm = TOKEN_TILE
    tok = lambda n: pl.BlockSpec((None, tm, n), lambda i, t: (i, t, 0))
    return pl.pallas_call(
        _merge_kernel,
        grid=(b, seq // tm),
        in_specs=[tok(HG_VAL), tok(HG_VAL), tok(ATT_Q), tok(d), tok(d), tok(d),
                  pl.BlockSpec((None, N_MOD, d), lambda i, t: (i, 0, 0)),
                  _const_spec((1, d)), _const_spec(wa.shape), _const_spec(wb.shape),
                  _const_spec(wo.shape), _const_spec(wr.shape)],
        out_specs=[tok(d), tok(d), tok(N_EXPERTS)],
        out_shape=[jax.ShapeDtypeStruct((b, seq, d), F32),
                   jax.ShapeDtypeStruct((b, seq, d), BF16),
                   jax.ShapeDtypeStruct((b, seq, N_EXPERTS), F32)],
        compiler_params=_params(("parallel", "parallel")),
        name="merge",
    )(on, sg, oatt, sga, sgb, x, mods, norm2_g, wa, wb, wo, wr)


def _route_kernel(lg_ref, pos_ref, gate_ref, *, cap):
    lg = lg_ref[...]
    n_e, length = lg.shape
    e = jnp.exp(lg - jnp.max(lg, axis=0, keepdims=True))
    aff = e / jnp.sum(e, axis=0, keepdims=True)
    bits = lax.bitcast_convert_type(aff, I32)

    def count(mask):
        return jnp.sum(jnp.where(mask, 1.0, 0.0), axis=1, keepdims=True)

    def value_step(i, t):
        cand = t | lax.shift_left(jnp.int32(1), jnp.asarray(30 - i, I32))
        return jnp.where(count(bits >= cand) >= cap, cand, t)

    thr = lax.fori_loop(0, 31, value_step, jnp.zeros((n_e, 1), I32))
    above = bits > thr
    tied = bits == thr
    need = cap - count(above)
    idx = lax.broadcasted_iota(I32, (1, length), 1)
    n_bits = int(np.log2(length))

    def index_step(i, j):
        cand = j | lax.shift_left(jnp.int32(1), jnp.asarray(n_bits - 1 - i, I32))
        return jnp.where(count(tied & (idx < cand)) < need, cand, j)

    last = lax.fori_loop(0, n_bits, index_step, jnp.zeros((n_e, 1), I32))
    sel = above | (tied & (idx <= last))

    r = lax.broadcasted_iota(I32, (LANES, LANES), 0)
    cc = lax.broadcasted_iota(I32, (LANES, LANES), 1)
    before = jnp.where(r < cc, 1.0, 0.0).astype(BF16)
    sel_b = jnp.where(sel, 1.0, 0.0).astype(BF16)
    offset = jnp.zeros((n_e, 1), F32)
    pieces = []
    for gidx in range(length // LANES):
        blk = sel_b[:, gidx * LANES:(gidx + 1) * LANES]
        pieces.append(jnp.dot(blk, before, preferred_element_type=F32) + offset)
        offset = offset + jnp.sum(blk.astype(F32), axis=1, keepdims=True)
    pos = jnp.concatenate(pieces, axis=1)
    pos_ref[...] = jnp.where(sel, pos.astype(I32), -1)
    gate_ref[...] = jnp.where(sel, aff, 0.0)


def _route(logits_t, cap):
    b, n_e, length = logits_t.shape
    spec = pl.BlockSpec((None, n_e, length), lambda i: (i, 0, 0))
    return pl.pallas_call(
        functools.partial(_route_kernel, cap=cap),
        grid=(b,),
        in_specs=[spec],
        out_specs=[spec, spec],
        out_shape=[jax.ShapeDtypeStruct((b, n_e, length), I32),
                   jax.ShapeDtypeStruct((b, n_e, length), F32)],
        compiler_params=_params(("parallel",)),
        name="route",
    )(logits_t)


def _gather_kernel(pos_ref, h_ref, xe_ref):
    ge, cap, d = xe_ref.shape
    length = h_ref.shape[0]
    slot = lax.broadcasted_iota(I32, (cap, length), 0)
    base = pl.program_id(1) * ge
    pieces = [jnp.where(pos_ref[pl.ds(base + e, 1), :] == slot, 1.0, 0.0).astype(BF16)
              for e in range(ge)]
    onehot = jnp.concatenate(pieces, axis=0)
    xe = jnp.dot(onehot, h_ref[...], preferred_element_type=F32)
    xe_ref[...] = xe.astype(BF16).reshape(ge, cap, d)


def _gather(pos, h2, cap):
    b, n_e, length = pos.shape
    d = h2.shape[-1]
    ge = GATHER_EXPERT_GROUP
    return pl.pallas_call(
        _gather_kernel,
        grid=(b, n_e // ge),
        in_specs=[pl.BlockSpec((None, n_e, length), lambda i, e: (i, 0, 0)),
                  pl.BlockSpec((None, length, d), lambda i, e: (i, 0, 0))],
        out_specs=pl.BlockSpec((None, ge, cap, d), lambda i, e: (i, e, 0, 0)),
        out_shape=jax.ShapeDtypeStruct((b, n_e, cap, d), BF16),
        compiler_params=_params(("parallel", "arbitrary")),
        name="gather",
    )(pos, h2)


def _ffn_kernel(xe_ref, wg_ref, wu_ref, wd_ref, ye_ref):
    nb, cap, d = xe_ref.shape
    x = xe_ref[...].reshape(nb * cap, d)
    hg = jnp.dot(x, wg_ref[...], preferred_element_type=F32)
    hu = jnp.dot(x, wu_ref[...], preferred_element_type=F32)
    hid = (hg * jax.nn.sigmoid(hg) * hu).astype(BF16)
    ye = jnp.dot(hid, wd_ref[...], preferred_element_type=F32)
    ye_ref[...] = ye.astype(BF16).reshape(nb, cap, d)


def _ffn(xe, wg, wu, wd):
    b, n_e, cap, d = xe.shape
    ff = wg.shape[-1]
    nb = int(np.gcd(b, FFN_BATCH_GROUP))
    tok = pl.BlockSpec((nb, None, cap, d), lambda e, i: (i, e, 0, 0))
    return pl.pallas_call(
        _ffn_kernel,
        grid=(n_e, b // nb),
        in_specs=[tok,
                  pl.BlockSpec((None, d, ff), lambda e, i: (e, 0, 0)),
                  pl.BlockSpec((None, d, ff), lambda e, i: (e, 0, 0)),
                  pl.BlockSpec((None, ff, d), lambda e, i: (e, 0, 0))],
        out_specs=tok,
        out_shape=jax.ShapeDtypeStruct((b, n_e, cap, d), BF16),
        compiler_params=_params(("parallel", "arbitrary")),
        name="expert_ffn",
    )(xe, wg, wu, wd)


def _combine_kernel(pos_ref, gate_ref, ye_ref, x1_ref, mod_ref, fg_ref, o_ref):
    n_e, cap, d = ye_ref.shape
    tm = x1_ref.shape[0]
    pos = pos_ref[...]
    gate = gate_ref[...]
    slot = lax.broadcasted_iota(I32, (tm, cap), 1)
    pieces = [jnp.where(pos[:, e:e + 1] == slot, gate[:, e:e + 1], 0.0).astype(BF16)
              for e in range(n_e)]
    scatter = jnp.concatenate(pieces, axis=1)
    y = jnp.dot(scatter, ye_ref[...].reshape(n_e * cap, d), preferred_element_type=F32)
    x2 = x1_ref[...] + mod_ref[5:6, :] * y
    ms = jnp.mean(x2 * x2, axis=-1, keepdims=True)
    o_ref[...] = x2 * lax.rsqrt(ms + NORM_EPS) * fg_ref[...]


def _combine(pos_t, gate_t, ye, x1, mods, final_g):
    b, seq, d = x1.shape
    n_e, cap = ye.shape[1], ye.shape[2]
    tm = COMBINE_TILE
    tok = lambda n: pl.BlockSpec((None, tm, n), lambda i, t: (i, t, 0))
    return pl.pallas_call(
        _combine_kernel,
        grid=(b, seq // tm),
        in_specs=[tok(n_e), tok(n_e),
                  pl.BlockSpec((None, n_e, cap, d), lambda i, t: (i, 0, 0, 0)),
                  tok(d),
                  pl.BlockSpec((None, N_MOD, d), lambda i, t: (i, 0, 0)),
                  _const_spec((1, d))],
        out_specs=tok(d),
        out_shape=jax.ShapeDtypeStruct((b, seq, d), F32),
        compiler_params=_params(("parallel", "arbitrary")),
        name="combine",
    )(pos_t, gate_t, ye, x1, mods, final_g)


def _rope_tables(length):
    rows = length // GRID_W
    row = jnp.repeat(jnp.arange(rows, dtype=F32), GRID_W)
    col = jnp.tile(jnp.arange(GRID_W, dtype=F32), rows)
    inv_freq = ROPE_THETA ** (-jnp.arange(0, ROPE_AXIS_DIM, 2, dtype=F32) / ROPE_AXIS_DIM)
    half = ROPE_AXIS_DIM // 2
    cos_parts, sin_parts = [], []
    for pos in (row, col):
        ang = pos[:, None] * inv_freq
        cos_parts += [jnp.cos(ang), jnp.cos(ang)]
        sin_parts += [-jnp.sin(ang), jnp.sin(ang)]
    cos = jnp.concatenate(cos_parts, axis=-1)
    sin = jnp.concatenate(sin_parts, axis=-1)
    reps = LANES // ATT_HEAD_DIM
    assert half * 4 == ATT_HEAD_DIM
    return jnp.tile(cos, (1, reps)), jnp.tile(sin, (1, reps))


def kernel(x, c, ctx, c_ctx, w_mod, b_mod, norm1_g, norm2_g, w_in, hg_lb_logits, hg_norm_g,
           q_norm_g, k_norm_g, w_branch_a, w_branch_b, w_out, w_router, w_exp_gate, w_exp_up,
           w_exp_down, final_norm_g):
    b, seq, d = x.shape
    depth = w_mod.shape[0]
    assert depth == 1, "context-stream update between layers is not implemented"
    layer = 0
    cap = EC_CAPACITY_FACTOR * seq // N_EXPERTS

    n_rows = -(-(b + 1) // SUBLANES) * SUBLANES
    cond = jnp.zeros((n_rows, d), F32).at[:b].set(c).at[b].set(c_ctx)
    mods = _adaln(cond, w_mod[layer], b_mod[layer]).reshape(n_rows, N_MOD, d)

    cos, sin = _rope_tables(seq)
    gsum = jnp.asarray(np.kron(np.eye(ATT_HEADS), np.ones((ATT_HEAD_DIM, ATT_HEAD_DIM))), BF16)
    qg = jnp.tile(q_norm_g[layer], ATT_HEADS).reshape(1, ATT_Q)
    kg = jnp.tile(k_norm_g[layer], ATT_KV_HEADS).reshape(1, ATT_KV)
    w_bf = w_in[layer].astype(BF16)
    w_ctx = jnp.concatenate([w_bf[:, _OFF_FF:_OFF_G], w_bf[:, _OFF_AK:_OFF_AV + ATT_KV]], axis=1)
    n1 = norm1_g[layer].reshape(1, d)

    hq, kf, lf, kb, lb, vi, sg, aq, ak, av, sga, sgb = _inproj_latent(
        x, mods, n1, w_bf, hg_lb_logits, qg, kg, cos, sin, gsum, layer)
    ckf, clf, ckb, clb, cvi, cak, cav = _inproj_context(
        ctx, mods, b, n1, w_ctx, hg_lb_logits, kg, gsum[:ATT_KV, :ATT_KV], layer)

    on = _hgrn(hq, kf, lf, kb, lb, vi, ckf, clf, ckb, clb, cvi,
               hg_norm_g[layer].reshape(1, HG_VAL))
    oatt = _attention(aq, ak, av, cak, cav)

    wr_hi = w_router[layer].astype(BF16)
    wr_lo = (w_router[layer] - wr_hi.astype(F32)).astype(BF16)
    wr = jnp.zeros((d, LANES), BF16).at[:, :N_EXPERTS].set(wr_hi)
    wr = wr.at[:, N_EXPERTS:2 * N_EXPERTS].set(wr_lo)
    x1, h2, logits = _merge(on, sg, oatt, sga, sgb, x, mods, norm2_g[layer].reshape(1, d),
                            w_branch_a[layer].astype(BF16), w_branch_b[layer].astype(BF16),
                            w_out[layer].astype(BF16), wr)

    pos, gate = _route(jnp.swapaxes(logits, 1, 2), cap)
    xe = _gather(pos, h2, cap)
    ye = _ffn(xe, w_exp_gate[layer].astype(BF16), w_exp_up[layer].astype(BF16),
              w_exp_down[layer].astype(BF16))
    return _combine(jnp.swapaxes(pos, 1, 2), jnp.swapaxes(gate, 1, 2), ye, x1, mods,
                    final_norm_g.reshape(1, d))
```

```python
import functools

import numpy as np
import jax
import jax.numpy as jnp
from jax import lax
from jax.experimental import pallas as pl
from jax.experimental.pallas import tpu as pltpu

F32 = jnp.float32
BF16 = jnp.bfloat16
I32 = jnp.int32

GRID_W = 64
HG_HEADS = 4
HG_DK = 128
HG_DV = 128
HG_KEY = HG_HEADS * HG_DK
HG_VAL = HG_HEADS * HG_DV
ATT_HEADS = 8
ATT_KV_HEADS = 2
ATT_HEAD_DIM = 64
ATT_GROUPS = ATT_HEADS // ATT_KV_HEADS
ATT_Q = ATT_HEADS * ATT_HEAD_DIM
ATT_KV = ATT_KV_HEADS * ATT_HEAD_DIM
ROPE_AXIS_DIM = ATT_HEAD_DIM // 2
ROPE_THETA = 10000.0
N_EXPERTS = 16
EC_CAPACITY_FACTOR = 2
N_MOD = 6
NORM_EPS = 1e-6
LOG2_E = float(np.log2(np.e))

LANES = 128
SUBLANES = 8
BF16_ROWS = 16
VMEM_LIMIT_BYTES = 56 * 1024 * 1024

HG_CHUNK = 64
HG_HEADS_PER_STEP = 4
TOKEN_TILE = 512
CTX_TILE = 256
Q_TILE = 512
Q_TILES_PER_STEP = 2
FFN_BATCH_GROUP = 4
ROUTE_TILE = 256
ROUTE_WINDOW = 64

_OFF_HQ = 0
_OFF_FF = _OFF_HQ + HG_KEY
_OFF_FB = _OFF_FF + HG_KEY
_OFF_I = _OFF_FB + HG_KEY
_OFF_G = _OFF_I + HG_VAL
_OFF_AQ = _OFF_G + HG_VAL
_OFF_AK = _OFF_AQ + ATT_Q
_OFF_AV = _OFF_AK + ATT_KV


def _params(sem):
    return pltpu.CompilerParams(dimension_semantics=sem, vmem_limit_bytes=VMEM_LIMIT_BYTES)


def _const_spec(shape):
    zeros = (0,) * len(shape)
    return pl.BlockSpec(shape, lambda *_: zeros)


def _adaln_kernel(c_ref, w_ref, b_ref, o_ref):
    c = c_ref[...]
    a = c * jax.nn.sigmoid(c)
    o_ref[...] = jnp.dot(a, w_ref[...], preferred_element_type=F32,
                         precision=lax.Precision.HIGHEST) + b_ref[...]


def _adaln(cond, w_mod, b_mod):
    n, d = cond.shape
    ncol = w_mod.shape[1]
    return pl.pallas_call(
        _adaln_kernel,
        grid=(ncol // d,),
        in_specs=[pl.BlockSpec((n, d), lambda j: (0, 0)),
                  pl.BlockSpec((d, d), lambda j: (0, j)),
                  pl.BlockSpec((1, d), lambda j: (0, j))],
        out_specs=pl.BlockSpec((n, d), lambda j: (0, j)),
        out_shape=jax.ShapeDtypeStruct((n, ncol), F32),
        compiler_params=_params(("arbitrary",)),
        name="adaln",
    )(cond, w_mod, b_mod.reshape(1, ncol))


def _rms_mod(x, gain, shift, scale):
    ms = jnp.mean(x * x, axis=-1, keepdims=True)
    h = x * lax.rsqrt(ms + NORM_EPS) * gain
    return h * (1.0 + scale) + shift


def _lower_bound(lbl_ref, direction, layer):
    lg = lbl_ref[direction]
    e = jnp.exp(lg - jnp.max(lg, axis=0, keepdims=True))
    p = e / jnp.sum(e, axis=0, keepdims=True)
    return jnp.sum(p[:layer + 1], axis=0, keepdims=True)


def _hgrn_gate(f_pre, lb):
    s = jax.nn.sigmoid(f_pre)
    return (1.0 - lb) * (1.0 - s), jnp.log2(lb + (1.0 - lb) * s)


def _head_rms(a, gsum, gain):
    ssq = jnp.dot((a * a).astype(BF16), gsum, preferred_element_type=F32)
    return a * lax.rsqrt(ssq * (1.0 / ATT_HEAD_DIM) + NORM_EPS) * gain


def _rope(a, cos, sin):
    lane = lax.broadcasted_iota(I32, (1, LANES), 1)
    first = (lane % ROPE_AXIS_DIM) < (ROPE_AXIS_DIM // 2)
    half = ROPE_AXIS_DIM // 2
    outs = []
    for j in range(a.shape[1] // LANES):
        xg = a[:, j * LANES:(j + 1) * LANES]
        partner = jnp.where(first, pltpu.roll(xg, LANES - half, 1), pltpu.roll(xg, half, 1))
        outs.append(xg * cos + partner * sin)
    return outs[0] if len(outs) == 1 else jnp.concatenate(outs, axis=1)


def _inproj_latent_kernel(x_ref, mod_ref, g_ref, w_ref, lbl_ref, qg_ref, kg_ref, cos_ref, sin_ref,
                          gs_ref, hq_ref, kf_ref, lf_ref, kb_ref, lb_ref, vi_ref, sg_ref,
                          aq_ref, ak_ref, av_ref, ga_ref, gb_ref, *, layer):
    hb = _rms_mod(x_ref[...], g_ref[...], mod_ref[0:1, :], mod_ref[1:2, :]).astype(BF16)

    def proj(lo, n):
        return jnp.dot(hb, w_ref[:, lo:lo + n], preferred_element_type=F32)

    hq_ref[...] = proj(_OFF_HQ, HG_KEY).astype(BF16)
    key, logf = _hgrn_gate(proj(_OFF_FF, HG_KEY), _lower_bound(lbl_ref, 0, layer))
    kf_ref[...] = key.astype(BF16)
    lf_ref[...] = logf
    key, logf = _hgrn_gate(proj(_OFF_FB, HG_KEY), _lower_bound(lbl_ref, 1, layer))
    kb_ref[...] = key.astype(BF16)
    lb_ref[...] = logf
    vi_ref[...] = proj(_OFF_I, HG_VAL).astype(BF16)
    g = proj(_OFF_G, HG_VAL)
    sg_ref[...] = (g * jax.nn.sigmoid(g)).astype(BF16)
    cos = cos_ref[...]
    sin = sin_ref[...]
    aq = _head_rms(proj(_OFF_AQ, ATT_Q), gs_ref[...], qg_ref[...])
    aq_ref[...] = (_rope(aq, cos, sin) * (ATT_HEAD_DIM ** -0.5 * LOG2_E)).astype(BF16)
    ak = _head_rms(proj(_OFF_AK, ATT_KV), gs_ref[0:ATT_KV, 0:ATT_KV], kg_ref[...])
    ak_ref[...] = _rope(ak, cos, sin).astype(BF16)
    av_ref[...] = proj(_OFF_AV, ATT_KV).astype(BF16)
    d = ga_ref.shape[-1]
    ga_ref[...] = jax.nn.sigmoid(proj(_OFF_AV + ATT_KV, d)).astype(BF16)
    gb_ref[...] = jax.nn.sigmoid(proj(_OFF_AV + ATT_KV + d, d)).astype(BF16)


def _inproj_context_kernel(x_ref, mod_ref, g_ref, w_ref, lbl_ref, kg_ref, gs_ref,
                           kf_ref, lf_ref, kb_ref, lb_ref, vi_ref, ak_ref, av_ref, *, layer):
    hb = _rms_mod(x_ref[...], g_ref[...], mod_ref[0:1, :], mod_ref[1:2, :]).astype(BF16)

    def proj(lo, n):
        return jnp.dot(hb, w_ref[:, lo:lo + n], preferred_element_type=F32)

    key, logf = _hgrn_gate(proj(0, HG_KEY), _lower_bound(lbl_ref, 0, layer))
    kf_ref[...] = key.astype(BF16)
    lf_ref[...] = logf
    key, logf = _hgrn_gate(proj(HG_KEY, HG_KEY), _lower_bound(lbl_ref, 1, layer))
    kb_ref[...] = key.astype(BF16)
    lb_ref[...] = logf
    vi_ref[...] = proj(2 * HG_KEY, HG_VAL).astype(BF16)
    ak = _head_rms(proj(2 * HG_KEY + HG_VAL, ATT_KV), gs_ref[...], kg_ref[...])
    ak_ref[...] = ak.astype(BF16)
    av_ref[...] = proj(2 * HG_KEY + HG_VAL + ATT_KV, ATT_KV).astype(BF16)


def _inproj_latent(x, mods, norm_g, w_bf, lbl, qg, kg, cos, sin, gsum, layer):
    b, seq, d = x.shape
    tm = TOKEN_TILE
    tok = lambda n: pl.BlockSpec((None, tm, n), lambda i, t: (i, t, 0))
    shp = lambda n, dt: jax.ShapeDtypeStruct((b, seq, n), dt)
    return pl.pallas_call(
        functools.partial(_inproj_latent_kernel, layer=layer),
        grid=(b, seq // tm),
        in_specs=[tok(d),
                  pl.BlockSpec((None, N_MOD, d), lambda i, t: (i, 0, 0)),
                  _const_spec((1, d)),
                  _const_spec(w_bf.shape),
                  _const_spec(lbl.shape),
                  _const_spec(qg.shape),
                  _const_spec(kg.shape),
                  pl.BlockSpec((tm, LANES), lambda i, t: (t, 0)),
                  pl.BlockSpec((tm, LANES), lambda i, t: (t, 0)),
                  _const_spec(gsum.shape)],
        out_specs=[tok(HG_KEY), tok(HG_KEY), tok(HG_KEY), tok(HG_KEY), tok(HG_KEY), tok(HG_VAL),
                   tok(HG_VAL), tok(ATT_Q), tok(ATT_KV), tok(ATT_KV), tok(d), tok(d)],
        out_shape=[shp(HG_KEY, BF16), shp(HG_KEY, BF16), shp(HG_KEY, F32), shp(HG_KEY, BF16),
                   shp(HG_KEY, F32), shp(HG_VAL, BF16), shp(HG_VAL, BF16), shp(ATT_Q, BF16),
                   shp(ATT_KV, BF16), shp(ATT_KV, BF16), shp(d, BF16), shp(d, BF16)],
        compiler_params=_params(("parallel", "parallel")),
        name="inproj_latent",
    )(x, mods, norm_g, w_bf, lbl, qg, kg, cos, sin, gsum)


def _inproj_context(ctx, mods, ctx_row, norm_g, w_bf, lbl, kg, gsum, layer):
    b, n_ctx, d = ctx.shape
    tm = CTX_TILE
    tok = lambda n: pl.BlockSpec((None, tm, n), lambda i, t: (i, t, 0))
    shp = lambda n, dt: jax.ShapeDtypeStruct((b, n_ctx, n), dt)
    return pl.pallas_call(
        functools.partial(_inproj_context_kernel, layer=layer),
        grid=(b, n_ctx // tm),
        in_specs=[tok(d),
                  pl.BlockSpec((None, N_MOD, d), lambda i, t: (ctx_row, 0, 0)),
                  _const_spec((1, d)),
                  _const_spec(w_bf.shape),
                  _const_spec(lbl.shape),
                  _const_spec(kg.shape),
                  _const_spec(gsum.shape)],
        out_specs=[tok(HG_KEY), tok(HG_KEY), tok(HG_KEY), tok(HG_KEY), tok(HG_VAL),
                   tok(ATT_KV), tok(ATT_KV)],
        out_shape=[shp(HG_KEY, BF16), shp(HG_KEY, F32), shp(HG_KEY, BF16), shp(HG_KEY, F32),
                   shp(HG_VAL, BF16), shp(ATT_KV, BF16), shp(ATT_KV, BF16)],
        compiler_params=_params(("parallel", "parallel")),
        name="inproj_context",
    )(ctx, mods, norm_g, w_bf, lbl, kg, gsum)


_HG_LEVELS = tuple(2 ** i for i in range(int(np.log2(HG_CHUNK))))


def _level_operand(h, ch):
    c = HG_CHUNK
    g, cum, cum_ref, reverse = ch["g"], ch["cum"], ch["cum_ref"], ch["reverse"]
    q32, k32 = ch["q32"], ch["k32"]
    if h >= SUBLANES:
        src, arg = [], []
        for p in range(c // (2 * h)):
            first = slice(p * 2 * h, p * 2 * h + h)
            second = slice(p * 2 * h + h, (p + 1) * 2 * h)
            r = p * 2 * h + (h if reverse else h - 1)
            mid = cum_ref[r:r + 1, :]
            if reverse:
                src += [q32[first], k32[second]]
                arg += [cum[first] - mid, mid - cum[second]]
            else:
                src += [k32[first], q32[second]]
                arg += [mid - cum[first], cum[second] - mid]
        return (jnp.concatenate(src, axis=0) * jnp.exp2(jnp.concatenate(arg, axis=0))).astype(BF16)
    row = lax.broadcasted_iota(I32, (c, 1), 0)
    u = row % (2 * h)
    query = (u < h) if reverse else (u >= h)
    src = jnp.where(query, q32, k32)
    if h == 1:
        arg = jnp.where(query, g, 0.0)
    elif h == 2:
        up = pltpu.roll(g, c - 1, 0)
        down = pltpu.roll(g, 1, 0)
        if reverse:
            arg = jnp.where(u == 0, g + up, jnp.where(u == 1, g, jnp.where(u == 2, 0.0, down)))
        else:
            arg = jnp.where(u == 0, up, jnp.where(u == 1, 0.0, jnp.where(u == 2, g, g + down)))
    else:
        pieces = []
        for p in range(c // (2 * h)):
            r = p * 2 * h + (h if reverse else h - 1)
            pieces.append(jnp.broadcast_to(cum_ref[r:r + 1, :], (2 * h, HG_DK)))
        arg = -jnp.abs(cum - jnp.concatenate(pieces, axis=0))
    return (src * jnp.exp2(arg)).astype(BF16)


def _hgrn_chunks(chains, tri, level_id, want_o):
    c = HG_CHUNK
    nt = (((1,), (1,)), ((), ()))
    tn = (((0,), (0,)), ((), ()))
    for ch in chains:
        g = ch["g"]
        g_hi = g.astype(BF16)
        g_lo = (g - g_hi.astype(F32)).astype(BF16)
        t = tri[ch["reverse"]]
        ch["cum"] = (jnp.dot(t, g_hi, preferred_element_type=F32)
                     + jnp.dot(t, g_lo, preferred_element_type=F32))
    for ch in chains:
        cum = ch["cum"]
        last = cum[0:1, :] if ch["reverse"] else cum[c - 1:c, :]
        ch["k32"] = ch["k"].astype(F32)
        kl = (ch["k32"] * jnp.exp2(last - cum)).astype(BF16)
        ch["st"] = ch["st_ref"][...]
        ch["st_ref"][...] = ch["st"] * jnp.exp2(last) + lax.dot_general(
            ch["v"], kl, tn, preferred_element_type=F32)
    if not want_o:
        return None
    for ch in chains:
        ch["q32"] = ch["q"].astype(F32)
        ch["cum_ref"][...] = ch["cum"]
        ch["scores"] = jnp.zeros((c, c), F32)
    for li, h in enumerate(_HG_LEVELS):
        for ch in chains:
            m = _level_operand(h, ch)
            s_h = lax.dot_general(m, m, nt, preferred_element_type=F32)
            ch["scores"] = jnp.where(level_id[ch["reverse"]] == li, s_h, ch["scores"])
    outs = []
    for ch in chains:
        self_term = jnp.sum(ch["q32"] * ch["k32"], axis=-1, keepdims=True)
        scores = jnp.where(level_id[ch["reverse"]] == len(_HG_LEVELS), self_term, ch["scores"])
        o = jnp.dot(scores.astype(BF16), ch["v"], preferred_element_type=F32)
        o = o + lax.dot_general((ch["q32"] * jnp.exp2(ch["cum"])).astype(BF16),
                                ch["st"].astype(BF16), nt, preferred_element_type=F32)
        outs.append(o)
    return outs


def _hgrn_kernel(q_ref, kf_ref, lf_ref, kb_ref, lb_ref, v_ref,
                 ckf_ref, clf_ref, ckb_ref, clb_ref, cv_ref, gn_ref, o_ref,
                 st_ref, of_ref, ob_ref, cum_ref):
    c = HG_CHUNK
    n_heads = q_ref.shape[1] // HG_DK
    n_ctx = ckf_ref.shape[0] // c
    n_lat = q_ref.shape[0] // c
    row = lax.broadcasted_iota(I32, (c, c), 0)
    col = lax.broadcasted_iota(I32, (c, c), 1)
    tri = {False: jnp.where(col <= row, 1.0, 0.0).astype(BF16),
           True: jnp.where(col >= row, 1.0, 0.0).astype(BF16)}
    x = row ^ col
    lvl = jnp.full((c, c), len(_HG_LEVELS), I32)
    for li, h in enumerate(_HG_LEVELS):
        lvl = jnp.where((x >= h) & (x < 2 * h), li, lvl)
    level_id = {False: jnp.where(col <= row, lvl, -1), True: jnp.where(col >= row, lvl, -1)}

    st_ref[...] = jnp.zeros_like(st_ref)

    def chains_at(a, z, k_f, l_f, k_b, l_b, v, q):
        out = []
        for hd in range(n_heads):
            lanes = slice(hd * HG_DK, (hd + 1) * HG_DK)
            for rev, start, kk, ll in ((False, a, k_f, l_f), (True, z, k_b, l_b)):
                rows = pl.ds(start, c)
                out.append(dict(q=None if q is None else q[rows, lanes], k=kk[rows, lanes],
                                v=v[rows, lanes], g=ll[rows, lanes], reverse=rev,
                                st_ref=st_ref.at[2 * hd + int(rev)],
                                cum_ref=cum_ref.at[2 * hd + int(rev)]))
        return out

    def ctx_body(j, carry):
        a = pl.multiple_of(j * c, c)
        z = pl.multiple_of((n_ctx - 1 - j) * c, c)
        _hgrn_chunks(chains_at(a, z, ckf_ref, clf_ref, ckb_ref, clb_ref, cv_ref, None),
                     tri, level_id, False)
        return carry

    lax.fori_loop(0, n_ctx, ctx_body, 0)

    def lat_body(j, carry):
        a = pl.multiple_of(j * c, c)
        z = pl.multiple_of((n_lat - 1 - j) * c, c)
        outs = _hgrn_chunks(chains_at(a, z, kf_ref, lf_ref, kb_ref, lb_ref, v_ref, q_ref),
                            tri, level_id, True)
        for hd in range(n_heads):
            lanes = slice(hd * HG_DV, (hd + 1) * HG_DV)
            of_ref[pl.ds(a, c), lanes] = outs[2 * hd]
            ob_ref[pl.ds(z, c), lanes] = outs[2 * hd + 1]
        return carry

    lax.fori_loop(0, n_lat, lat_body, 0)

    for hd in range(n_heads):
        lanes = slice(hd * HG_DV, (hd + 1) * HG_DV)
        o = of_ref[:, lanes] + ob_ref[:, lanes]
        ms = jnp.mean(o * o, axis=-1, keepdims=True)
        o_ref[:, lanes] = (o * lax.rsqrt(ms + NORM_EPS) * gn_ref[:, lanes]).astype(BF16)


def _hgrn(hq, kf, lf, kb, lb, vi, ckf, clf, ckb, clb, cvi, gn):
    b, seq, _ = hq.shape
    n_ctx = ckf.shape[1]
    nh = HG_HEADS_PER_STEP
    lat = pl.BlockSpec((None, seq, nh * HG_DK), lambda i, h: (i, 0, h))
    cx = pl.BlockSpec((None, n_ctx, nh * HG_DK), lambda i, h: (i, 0, h))
    return pl.pallas_call(
        _hgrn_kernel,
        grid=(b, HG_HEADS // nh),
        in_specs=[lat, lat, lat, lat, lat, lat, cx, cx, cx, cx, cx,
                  pl.BlockSpec((1, nh * HG_DV), lambda i, h: (0, h))],
        out_specs=lat,
        out_shape=jax.ShapeDtypeStruct((b, seq, HG_VAL), BF16),
        scratch_shapes=[pltpu.VMEM((2 * nh, HG_DV, HG_DK), F32),
                        pltpu.VMEM((seq, nh * HG_DV), F32), pltpu.VMEM((seq, nh * HG_DV), F32),
                        pltpu.VMEM((2 * nh, HG_CHUNK, HG_DK), F32)],
        compiler_params=_params(("parallel", "parallel")),
        name="hgrn_scan",
    )(hq, kf, lf, kb, lb, vi, ckf, clf, ckb, clb, cvi, gn)


_HEADS_PER_TILE = LANES // ATT_HEAD_DIM


def _attn_kernel(q_ref, kx_ref, vx_ref, kc_ref, vc_ref, o_ref, kp_ref, vp_ref):
    kvh = pl.program_id(1)

    @pl.when(pl.program_id(2) == 0)
    def _build():
        k_all = jnp.concatenate([kc_ref[...], kx_ref[...]], axis=0)
        v_all = jnp.concatenate([vc_ref[...], vx_ref[...]], axis=0)
        r = lax.broadcasted_iota(I32, (ATT_KV, LANES), 0)
        cidx = lax.broadcasted_iota(I32, (ATT_KV, LANES), 1)
        for u in range(_HEADS_PER_TILE):
            place = (cidx // ATT_HEAD_DIM == u) & (r == kvh * ATT_HEAD_DIM + cidx - u * ATT_HEAD_DIM)
            rep = jnp.where(place, 1.0, 0.0).astype(BF16)
            kp_ref[u] = jnp.dot(k_all, rep, preferred_element_type=F32).astype(BF16)
            vp_ref[u] = jnp.dot(v_all, rep, preferred_element_type=F32).astype(BF16)

    for t in range(q_ref.shape[0] // Q_TILE):
        rows = slice(t * Q_TILE, (t + 1) * Q_TILE)
        for pair in range(ATT_GROUPS // _HEADS_PER_TILE):
            lanes = slice(pair * LANES, (pair + 1) * LANES)
            q = q_ref[rows, lanes]
            acc = jnp.zeros(q.shape, F32)
            for u in range(_HEADS_PER_TILE):
                s = lax.dot_general(q, kp_ref[u], (((1,), (1,)), ((), ())),
                                    preferred_element_type=F32)
                p = jnp.exp2(s - jnp.max(s, axis=-1, keepdims=True))
                inv = 1.0 / jnp.sum(p, axis=-1, keepdims=True)
                acc = acc + jnp.dot(p.astype(BF16), vp_ref[u], preferred_element_type=F32) * inv
            o_ref[rows, lanes] = acc.astype(BF16)


def _attention(aq, ak, av, cak, cav):
    b, seq, _ = aq.shape
    n_ctx = cak.shape[1]
    width = ATT_GROUPS * ATT_HEAD_DIM
    kx = pl.BlockSpec((None, seq, ATT_KV), lambda i, h, t: (i, 0, 0))
    kc = pl.BlockSpec((None, n_ctx, ATT_KV), lambda i, h, t: (i, 0, 0))
    rows = Q_TILE * Q_TILES_PER_STEP
    qo = pl.BlockSpec((None, rows, width), lambda i, h, t: (i, t, h))
    return pl.pallas_call(
        _attn_kernel,
        grid=(b, ATT_KV_HEADS, seq // rows),
        in_specs=[qo, kx, kx, kc, kc],
        out_specs=qo,
        out_shape=jax.ShapeDtypeStruct((b, seq, ATT_Q), BF16),
        scratch_shapes=[pltpu.VMEM((_HEADS_PER_TILE, seq + n_ctx, LANES), BF16),
                        pltpu.VMEM((_HEADS_PER_TILE, seq + n_ctx, LANES), BF16)],
        compiler_params=_params(("parallel", "parallel", "arbitrary")),
        name="attention",
    )(aq, ak, av, cak, cav)


def _merge_kernel(on_ref, sg_ref, oa_ref, ga_ref, gb_ref, x_ref, mod_ref, n2_ref,
                  wa_ref, wb_ref, wo_ref, wr_ref, x1_ref, h2_ref, lg_ref):
    a_in = (on_ref[...].astype(F32) * sg_ref[...].astype(F32)).astype(BF16)
    a = jnp.dot(a_in, wa_ref[...], preferred_element_type=F32)
    bb = jnp.dot(oa_ref[...], wb_ref[...], preferred_element_type=F32)
    m = ga_ref[...].astype(F32) * a + gb_ref[...].astype(F32) * bb
    y = jnp.dot(m.astype(BF16), wo_ref[...], preferred_element_type=F32)
    x1 = x_ref[...] + mod_ref[2:3, :] * y
    x1_ref[...] = x1
    h2 = _rms_mod(x1, n2_ref[...], mod_ref[3:4, :], mod_ref[4:5, :])
    h_hi = h2.astype(BF16)
    h_lo = (h2 - h_hi.astype(F32)).astype(BF16)
    h2_ref[...] = h_hi
    wr = wr_ref[...]
    r = (jnp.dot(h_hi, wr, preferred_element_type=F32)
         + jnp.dot(h_lo, wr, preferred_element_type=F32))
    lg_ref[...] = r[:, :N_EXPERTS] + r[:, N_EXPERTS:2 * N_EXPERTS]


def _merge(on, sg, oatt, sga, sgb, x, mods, norm2_g, wa, wb, wo, wr):
    b, seq, d = x.shape
    tm = TOKEN_TILE
    tok = lambda n: pl.BlockSpec((None, tm, n), lambda i, t: (i, t, 0))
    return pl.pallas_call(
        _merge_kernel,
        grid=(b, seq // tm),
        in_specs=[tok(HG_VAL), tok(HG_VAL), tok(ATT_Q), tok(d), tok(d), tok(d),
                  pl.BlockSpec((None, N_MOD, d), lambda i, t: (i, 0, 0)),
                  _const_spec((1, d)), _const_spec(wa.shape), _const_spec(wb.shape),
                  _const_spec(wo.shape), _const_spec(wr.shape)],
        out_specs=[tok(d), tok(d), tok(N_EXPERTS)],
        out_shape=[jax.ShapeDtypeStruct((b, seq, d), F32),
                   jax.ShapeDtypeStruct((b, seq, d), BF16),
                   jax.ShapeDtypeStruct((b, seq, N_EXPERTS), F32)],
        compiler_params=_params(("parallel", "parallel")),
        name="merge",
    )(on, sg, oatt, sga, sgb, x, mods, norm2_g, wa, wb, wo, wr)


def _route_kernel(lg_ref, pos_ref, gate_ref, starts_ref, *, cap):
    lg = lg_ref[...]
    n_e, length = lg.shape
    e = jnp.exp(lg - jnp.max(lg, axis=0, keepdims=True))
    aff = e / jnp.sum(e, axis=0, keepdims=True)
    bits = lax.bitcast_convert_type(aff, I32)

    def count(mask):
        return jnp.sum(jnp.where(mask, 1.0, 0.0), axis=1, keepdims=True)

    def value_step(i, t):
        cand = t | lax.shift_left(jnp.int32(1), jnp.asarray(30 - i, I32))
        return jnp.where(count(bits >= cand) >= cap, cand, t)

    thr = lax.fori_loop(0, 31, value_step, jnp.zeros((n_e, 1), I32))
    above = bits > thr
    tied = bits == thr
    need = cap - count(above)
    idx = lax.broadcasted_iota(I32, (1, length), 1)
    n_bits = int(np.log2(length))

    def index_step(i, j):
        cand = j | lax.shift_left(jnp.int32(1), jnp.asarray(n_bits - 1 - i, I32))
        return jnp.where(count(tied & (idx < cand)) < need, cand, j)

    last = lax.fori_loop(0, n_bits, index_step, jnp.zeros((n_e, 1), I32))
    sel = above | (tied & (idx <= last))

    r = lax.broadcasted_iota(I32, (LANES, LANES), 0)
    cc = lax.broadcasted_iota(I32, (LANES, LANES), 1)
    before = jnp.where(r < cc, 1.0, 0.0).astype(BF16)
    sel_b = jnp.where(sel, 1.0, 0.0).astype(BF16)
    offset = jnp.zeros((n_e, 1), F32)
    pieces = []
    groups_per_tile = ROUTE_TILE // LANES
    lane = lax.broadcasted_iota(I32, (1, LANES), 1)
    starts = jnp.zeros((n_e, LANES), F32)
    for gidx in range(length // LANES):
        blk = sel_b[:, gidx * LANES:(gidx + 1) * LANES]
        pieces.append(jnp.dot(blk, before, preferred_element_type=F32) + offset)
        offset = offset + jnp.sum(blk.astype(F32), axis=1, keepdims=True)
        if (gidx + 1) % groups_per_tile == 0:
            starts = jnp.where(lane == (gidx + 1) // groups_per_tile, offset, starts)
    pos = jnp.concatenate(pieces, axis=1)
    pos_ref[...] = jnp.where(sel, pos.astype(I32), -1)
    gate_ref[...] = jnp.where(sel, aff, 0.0)
    starts_ref[...] = starts.astype(I32)


def _route(logits_t, cap):
    b, n_e, length = logits_t.shape
    assert length // ROUTE_TILE < LANES
    spec = pl.BlockSpec((None, n_e, length), lambda i: (i, 0, 0))
    return pl.pallas_call(
        functools.partial(_route_kernel, cap=cap),
        grid=(b,),
        in_specs=[spec],
        out_specs=[spec, spec, pl.BlockSpec((None, n_e, LANES), lambda i: (i, 0, 0))],
        out_shape=[jax.ShapeDtypeStruct((b, n_e, length), I32),
                   jax.ShapeDtypeStruct((b, n_e, length), F32),
                   jax.ShapeDtypeStruct((b, n_e, LANES), I32)],
        compiler_params=_params(("parallel",)),
        name="route",
    )(logits_t)


def _tile_windows(starts_ref, sample, tile, n_e, n_tiles):
    stride = n_tiles + 1
    lows, n_win = [], jnp.int32(0)
    for e in range(n_e):
        at = (sample * n_e + e) * stride + tile
        low = (starts_ref[at] // BF16_ROWS) * BF16_ROWS
        lows.append(low)
        n_win = jnp.maximum(n_win, (starts_ref[at + 1] - low + ROUTE_WINDOW - 1) // ROUTE_WINDOW)
    return lows, n_win


def _window(low, w, cap):
    lo = low + w * ROUTE_WINDOW
    start = pl.multiple_of(jnp.minimum(lo, cap - ROUTE_WINDOW), BF16_ROWS)
    slot = start + lax.broadcasted_iota(I32, (ROUTE_WINDOW, 1), 0)
    return start, slot, slot >= lo


def _gather_kernel(starts_ref, pos_ref, h_ref, xe_ref):
    n_e, cap, d = xe_ref.shape
    n_tiles = h_ref.shape[0] // ROUTE_TILE
    sample = pl.program_id(0)
    xe_ref[...] = jnp.zeros_like(xe_ref)

    def tile_body(j, carry):
        t0 = pl.multiple_of(j * ROUTE_TILE, ROUTE_TILE)
        h_tile = h_ref[pl.ds(t0, ROUTE_TILE), :]
        lows, n_win = _tile_windows(starts_ref, sample, j, n_e, n_tiles)

        def window_body(w, c2):
            pieces, begins = [], []
            for e in range(n_e):
                start, slot, live = _window(lows[e], w, cap)
                hit = (pos_ref[e:e + 1, pl.ds(t0, ROUTE_TILE)] == slot) & live
                pieces.append(jnp.where(hit, 1.0, 0.0).astype(BF16))
                begins.append(start)
            onehot = jnp.concatenate(pieces, axis=0)
            rows = jnp.dot(onehot, h_tile, preferred_element_type=F32).astype(BF16)
            for e in range(n_e):
                dst = xe_ref.at[e, pl.ds(begins[e], ROUTE_WINDOW), :]
                dst[...] = dst[...] + rows[e * ROUTE_WINDOW:(e + 1) * ROUTE_WINDOW]
            return c2

        lax.fori_loop(0, n_win, window_body, 0)
        return carry

    lax.fori_loop(0, n_tiles, tile_body, 0)


def _gather(starts, pos, h2, cap):
    b, n_e, length = pos.shape
    d = h2.shape[-1]
    assert length % ROUTE_TILE == 0 and cap % BF16_ROWS == 0 and cap >= ROUTE_WINDOW
    grid_spec = pltpu.PrefetchScalarGridSpec(
        num_scalar_prefetch=1,
        grid=(b,),
        in_specs=[pl.BlockSpec((None, n_e, length), lambda i, st: (i, 0, 0)),
                  pl.BlockSpec((None, length, d), lambda i, st: (i, 0, 0))],
        out_specs=pl.BlockSpec((None, n_e, cap, d), lambda i, st: (i, 0, 0, 0)))
    return pl.pallas_call(
        _gather_kernel,
        grid_spec=grid_spec,
        out_shape=jax.ShapeDtypeStruct((b, n_e, cap, d), BF16),
        compiler_params=_params(("parallel",)),
        name="gather",
    )(starts, pos, h2)


def _ffn_kernel(xe_ref, wg_ref, wu_ref, wd_ref, ye_ref):
    nb, cap, d = xe_ref.shape
    x = xe_ref[...].reshape(nb * cap, d)
    hg = jnp.dot(x, wg_ref[...].astype(BF16), preferred_element_type=F32)
    hu = jnp.dot(x, wu_ref[...].astype(BF16), preferred_element_type=F32)
    hid = (hg * jax.nn.sigmoid(hg) * hu).astype(BF16)
    ye = jnp.dot(hid, wd_ref[...].astype(BF16), preferred_element_type=F32)
    ye_ref[...] = ye.astype(BF16).reshape(nb, cap, d)


def _ffn(xe, wg, wu, wd):
    b, n_e, cap, d = xe.shape
    ff = wg.shape[-1]
    nb = int(np.gcd(b, FFN_BATCH_GROUP))
    tok = pl.BlockSpec((nb, None, cap, d), lambda e, i: (i, e, 0, 0))
    return pl.pallas_call(
        _ffn_kernel,
        grid=(n_e, b // nb),
        in_specs=[tok,
                  pl.BlockSpec((None, d, ff), lambda e, i: (e, 0, 0)),
                  pl.BlockSpec((None, d, ff), lambda e, i: (e, 0, 0)),
                  pl.BlockSpec((None, ff, d), lambda e, i: (e, 0, 0))],
        out_specs=tok,
        out_shape=jax.ShapeDtypeStruct((b, n_e, cap, d), BF16),
        compiler_params=_params(("parallel", "arbitrary")),
        name="expert_ffn",
    )(xe, wg, wu, wd)


def _combine_kernel(starts_ref, pos_ref, gate_ref, ye_ref, x1_ref, mod_ref, fg_ref, o_ref):
    n_e, cap, d = ye_ref.shape
    tm = x1_ref.shape[0]
    n_tiles = pl.num_programs(1)
    lows, n_win = _tile_windows(starts_ref, pl.program_id(0), pl.program_id(1), n_e, n_tiles)

    def window_body(w, y):
        pieces, rows = [], []
        for e in range(n_e):
            start, slot, live = _window(lows[e], w, cap)
            hit = (pos_ref[e:e + 1, :] == slot) & live
            pieces.append(jnp.where(hit, gate_ref[e:e + 1, :], 0.0).astype(BF16))
            rows.append(ye_ref[e, pl.ds(start, ROUTE_WINDOW), :])
        scatter = jnp.concatenate(pieces, axis=0)
        return y + lax.dot_general(scatter, jnp.concatenate(rows, axis=0),
                                   (((0,), (0,)), ((), ())), preferred_element_type=F32)

    y = lax.fori_loop(0, n_win, window_body, jnp.zeros((tm, d), F32))
    x2 = x1_ref[...] + mod_ref[5:6, :] * y
    ms = jnp.mean(x2 * x2, axis=-1, keepdims=True)
    o_ref[...] = x2 * lax.rsqrt(ms + NORM_EPS) * fg_ref[...]


def _combine(starts, pos, gate, ye, x1, mods, final_g):
    b, seq, d = x1.shape
    n_e, cap = ye.shape[1], ye.shape[2]
    tm = ROUTE_TILE
    tok = lambda n: pl.BlockSpec((None, tm, n), lambda i, t, st: (i, t, 0))
    sel = pl.BlockSpec((None, n_e, tm), lambda i, t, st: (i, 0, t))
    grid_spec = pltpu.PrefetchScalarGridSpec(
        num_scalar_prefetch=1,
        grid=(b, seq // tm),
        in_specs=[sel, sel,
                  pl.BlockSpec((None, n_e, cap, d), lambda i, t, st: (i, 0, 0, 0)),
                  tok(d),
                  pl.BlockSpec((None, N_MOD, d), lambda i, t, st: (i, 0, 0)),
                  pl.BlockSpec((1, d), lambda i, t, st: (0, 0))],
        out_specs=tok(d))
    return pl.pallas_call(
        _combine_kernel,
        grid_spec=grid_spec,
        out_shape=jax.ShapeDtypeStruct((b, seq, d), F32),
        compiler_params=_params(("parallel", "arbitrary")),
        name="combine",
    )(starts, pos, gate, ye, x1, mods, final_g)


def _rope_tables(length):
    rows = length // GRID_W
    row = jnp.repeat(jnp.arange(rows, dtype=F32), GRID_W)
    col = jnp.tile(jnp.arange(GRID_W, dtype=F32), rows)
    inv_freq = ROPE_THETA ** (-jnp.arange(0, ROPE_AXIS_DIM, 2, dtype=F32) / ROPE_AXIS_DIM)
    half = ROPE_AXIS_DIM // 2
    cos_parts, sin_parts = [], []
    for pos in (row, col):
        ang = pos[:, None] * inv_freq
        cos_parts += [jnp.cos(ang), jnp.cos(ang)]
        sin_parts += [-jnp.sin(ang), jnp.sin(ang)]
    cos = jnp.concatenate(cos_parts, axis=-1)
    sin = jnp.concatenate(sin_parts, axis=-1)
    reps = LANES // ATT_HEAD_DIM
    assert half * 4 == ATT_HEAD_DIM
    return jnp.tile(cos, (1, reps)), jnp.tile(sin, (1, reps))


def kernel(x, c, ctx, c_ctx, w_mod, b_mod, norm1_g, norm2_g, w_in, hg_lb_logits, hg_norm_g,
           q_norm_g, k_norm_g, w_branch_a, w_branch_b, w_out, w_router, w_exp_gate, w_exp_up,
           w_exp_down, final_norm_g):
    b, seq, d = x.shape
    depth = w_mod.shape[0]
    assert depth == 1, "context-stream update between layers is not implemented"
    layer = 0
    cap = EC_CAPACITY_FACTOR * seq // N_EXPERTS

    n_rows = -(-(b + 1) // SUBLANES) * SUBLANES
    cond = jnp.zeros((n_rows, d), F32).at[:b].set(c).at[b].set(c_ctx)
    mods = _adaln(cond, w_mod[layer], b_mod[layer]).reshape(n_rows, N_MOD, d)

    cos, sin = _rope_tables(seq)
    gsum = jnp.asarray(np.kron(np.eye(ATT_HEADS), np.ones((ATT_HEAD_DIM, ATT_HEAD_DIM))), BF16)
    qg = jnp.tile(q_norm_g[layer], ATT_HEADS).reshape(1, ATT_Q)
    kg = jnp.tile(k_norm_g[layer], ATT_KV_HEADS).reshape(1, ATT_KV)
    w_bf = w_in[layer].astype(BF16)
    w_ctx = jnp.concatenate([w_bf[:, _OFF_FF:_OFF_G], w_bf[:, _OFF_AK:_OFF_AV + ATT_KV]], axis=1)
    n1 = norm1_g[layer].reshape(1, d)

    hq, kf, lf, kb, lb, vi, sg, aq, ak, av, sga, sgb = _inproj_latent(
        x, mods, n1, w_bf, hg_lb_logits, qg, kg, cos, sin, gsum, layer)
    ckf, clf, ckb, clb, cvi, cak, cav = _inproj_context(
        ctx, mods, b, n1, w_ctx, hg_lb_logits, kg, gsum[:ATT_KV, :ATT_KV], layer)

    on = _hgrn(hq, kf, lf, kb, lb, vi, ckf, clf, ckb, clb, cvi,
               hg_norm_g[layer].reshape(1, HG_VAL))
    oatt = _attention(aq, ak, av, cak, cav)

    wr_hi = w_router[layer].astype(BF16)
    wr_lo = (w_router[layer] - wr_hi.astype(F32)).astype(BF16)
    wr = jnp.zeros((d, LANES), BF16).at[:, :N_EXPERTS].set(wr_hi)
    wr = wr.at[:, N_EXPERTS:2 * N_EXPERTS].set(wr_lo)
    x1, h2, logits = _merge(on, sg, oatt, sga, sgb, x, mods, norm2_g[layer].reshape(1, d),
                            w_branch_a[layer].astype(BF16), w_branch_b[layer].astype(BF16),
                            w_out[layer].astype(BF16), wr)

    pos, gate, starts = _route(jnp.swapaxes(logits, 1, 2), cap)
    starts = starts[:, :, :seq // ROUTE_TILE + 1].reshape(-1)
    xe = _gather(starts, pos, h2, cap)
    ye = _ffn(xe, w_exp_gate[layer], w_exp_up[layer], w_exp_down[layer])
    return _combine(starts, pos, gate, ye, x1, mods, final_norm_g.reshape(1, d))
```

```python
import functools

import numpy as np
import jax
import jax.numpy as jnp
from jax import lax
from jax.experimental import pallas as pl
from jax.experimental.pallas import tpu as pltpu

F32 = jnp.float32
BF16 = jnp.bfloat16
I32 = jnp.int32

GRID_W = 64
HG_HEADS = 4
HG_DK = 128
HG_DV = 128
HG_KEY = HG_HEADS * HG_DK
HG_VAL = HG_HEADS * HG_DV
ATT_HEADS = 8
ATT_KV_HEADS = 2
ATT_HEAD_DIM = 64
ATT_GROUPS = ATT_HEADS // ATT_KV_HEADS
ATT_Q = ATT_HEADS * ATT_HEAD_DIM
ATT_KV = ATT_KV_HEADS * ATT_HEAD_DIM
ROPE_AXIS_DIM = ATT_HEAD_DIM // 2
ROPE_THETA = 10000.0
N_EXPERTS = 16
EC_CAPACITY_FACTOR = 2
N_MOD = 6
NORM_EPS = 1e-6
LOG2_E = float(np.log2(np.e))

LANES = 128
SUBLANES = 8
BF16_ROWS = 16
VMEM_LIMIT_BYTES = 56 * 1024 * 1024

HG_CHUNK = 64
HG_HEADS_PER_STEP = 4
HG_STACK = 1
TOKEN_TILE = 512
CTX_TILE = 256
MERGE_SUBTILES = 2
Q_TILE = 512
Q_TILES_PER_STEP = 2
FFN_BATCH_GROUP = 4
ROUTE_TILE = 256
ROUTE_WINDOW = 64

_OFF_HQ = 0
_OFF_FF = _OFF_HQ + HG_KEY
_OFF_FB = _OFF_FF + HG_KEY
_OFF_I = _OFF_FB + HG_KEY
_OFF_G = _OFF_I + HG_VAL
_OFF_AQ = _OFF_G + HG_VAL
_OFF_AK = _OFF_AQ + ATT_Q
_OFF_AV = _OFF_AK + ATT_KV


def _params(sem):
    return pltpu.CompilerParams(dimension_semantics=sem, vmem_limit_bytes=VMEM_LIMIT_BYTES)


def _const_spec(shape):
    zeros = (0,) * len(shape)
    return pl.BlockSpec(shape, lambda *_: zeros)


def _adaln_kernel(c_ref, w_ref, b_ref, o_ref):
    c = c_ref[...]
    a = c * jax.nn.sigmoid(c)
    o_ref[...] = jnp.dot(a, w_ref[...], preferred_element_type=F32,
                         precision=lax.Precision.HIGHEST) + b_ref[...]


def _adaln(cond, w_mod, b_mod):
    n, d = cond.shape
    ncol = w_mod.shape[1]
    return pl.pallas_call(
        _adaln_kernel,
        grid=(ncol // d,),
        in_specs=[pl.BlockSpec((n, d), lambda j: (0, 0)),
                  pl.BlockSpec((d, d), lambda j: (0, j)),
                  pl.BlockSpec((1, d), lambda j: (0, j))],
        out_specs=pl.BlockSpec((n, d), lambda j: (0, j)),
        out_shape=jax.ShapeDtypeStruct((n, ncol), F32),
        compiler_params=_params(("arbitrary",)),
        name="adaln",
    )(cond, w_mod, b_mod.reshape(1, ncol))


def _rms_mod(x, gain, shift, scale):
    ms = jnp.mean(x * x, axis=-1, keepdims=True)
    h = x * lax.rsqrt(ms + NORM_EPS) * gain
    return h * (1.0 + scale) + shift


def _lower_bound(lbl_ref, direction, layer):
    lg = lbl_ref[direction]
    e = jnp.exp(lg - jnp.max(lg, axis=0, keepdims=True))
    p = e / jnp.sum(e, axis=0, keepdims=True)
    return jnp.sum(p[:layer + 1], axis=0, keepdims=True)


def _hgrn_gate(f_pre, lb):
    s = jax.nn.sigmoid(f_pre)
    return (1.0 - lb) * (1.0 - s), jnp.log2(lb + (1.0 - lb) * s)


def _head_rms(a, gsum, gain):
    ssq = jnp.dot((a * a).astype(BF16), gsum, preferred_element_type=F32)
    return a * lax.rsqrt(ssq * (1.0 / ATT_HEAD_DIM) + NORM_EPS) * gain


def _rope(a, cos, sin):
    lane = lax.broadcasted_iota(I32, (1, LANES), 1)
    first = (lane % ROPE_AXIS_DIM) < (ROPE_AXIS_DIM // 2)
    half = ROPE_AXIS_DIM // 2
    outs = []
    for j in range(a.shape[1] // LANES):
        xg = a[:, j * LANES:(j + 1) * LANES]
        partner = jnp.where(first, pltpu.roll(xg, LANES - half, 1), pltpu.roll(xg, half, 1))
        outs.append(xg * cos + partner * sin)
    return outs[0] if len(outs) == 1 else jnp.concatenate(outs, axis=1)


def _inproj_latent_kernel(x_ref, mod_ref, g_ref, w_ref, lbl_ref, qg_ref, kg_ref, cos_ref, sin_ref,
                          gs_ref, hq_ref, kf_ref, lf_ref, kb_ref, lb_ref, vi_ref, sg_ref,
                          aq_ref, ak_ref, av_ref, ga_ref, gb_ref, *, layer):
    hb = _rms_mod(x_ref[...], g_ref[...], mod_ref[0:1, :], mod_ref[1:2, :]).astype(BF16)

    def proj(lo, n):
        return jnp.dot(hb, w_ref[:, lo:lo + n], preferred_element_type=F32)

    hq_ref[...] = proj(_OFF_HQ, HG_KEY).astype(BF16)
    key, logf = _hgrn_gate(proj(_OFF_FF, HG_KEY), _lower_bound(lbl_ref, 0, layer))
    kf_ref[...] = key.astype(BF16)
    lf_ref[...] = logf
    key, logf = _hgrn_gate(proj(_OFF_FB, HG_KEY), _lower_bound(lbl_ref, 1, layer))
    kb_ref[...] = key.astype(BF16)
    lb_ref[...] = logf
    vi_ref[...] = proj(_OFF_I, HG_VAL).astype(BF16)
    g = proj(_OFF_G, HG_VAL)
    sg_ref[...] = (g * jax.nn.sigmoid(g)).astype(BF16)
    cos = cos_ref[...]
    sin = sin_ref[...]
    aq = _head_rms(proj(_OFF_AQ, ATT_Q), gs_ref[...], qg_ref[...])
    aq_ref[...] = (_rope(aq, cos, sin) * (ATT_HEAD_DIM ** -0.5 * LOG2_E)).astype(BF16)
    ak = _head_rms(proj(_OFF_AK, ATT_KV), gs_ref[0:ATT_KV, 0:ATT_KV], kg_ref[...])
    ak_ref[...] = _rope(ak, cos, sin).astype(BF16)
    av_ref[...] = proj(_OFF_AV, ATT_KV).astype(BF16)
    d = ga_ref.shape[-1]
    ga_ref[...] = jax.nn.sigmoid(proj(_OFF_AV + ATT_KV, d)).astype(BF16)
    gb_ref[...] = jax.nn.sigmoid(proj(_OFF_AV + ATT_KV + d, d)).astype(BF16)


def _inproj_context_kernel(x_ref, mod_ref, g_ref, w_ref, lbl_ref, kg_ref, gs_ref,
                           kf_ref, lf_ref, kb_ref, lb_ref, vi_ref, ak_ref, av_ref, *, layer):
    hb = _rms_mod(x_ref[...], g_ref[...], mod_ref[0:1, :], mod_ref[1:2, :]).astype(BF16)

    def proj(lo, n):
        return jnp.dot(hb, w_ref[:, lo:lo + n], preferred_element_type=F32)

    key, logf = _hgrn_gate(proj(0, HG_KEY), _lower_bound(lbl_ref, 0, layer))
    kf_ref[...] = key.astype(BF16)
    lf_ref[...] = logf
    key, logf = _hgrn_gate(proj(HG_KEY, HG_KEY), _lower_bound(lbl_ref, 1, layer))
    kb_ref[...] = key.astype(BF16)
    lb_ref[...] = logf
    vi_ref[...] = proj(2 * HG_KEY, HG_VAL).astype(BF16)
    ak = _head_rms(proj(2 * HG_KEY + HG_VAL, ATT_KV), gs_ref[...], kg_ref[...])
    ak_ref[...] = ak.astype(BF16)
    av_ref[...] = proj(2 * HG_KEY + HG_VAL + ATT_KV, ATT_KV).astype(BF16)


def _inproj_latent(x, mods, norm_g, w_bf, lbl, qg, kg, cos, sin, gsum, layer):
    b, seq, d = x.shape
    tm = TOKEN_TILE
    tok = lambda n: pl.BlockSpec((None, tm, n), lambda i, t: (i, t, 0))
    shp = lambda n, dt: jax.ShapeDtypeStruct((b, seq, n), dt)
    return pl.pallas_call(
        functools.partial(_inproj_latent_kernel, layer=layer),
        grid=(b, seq // tm),
        in_specs=[tok(d),
                  pl.BlockSpec((None, N_MOD, d), lambda i, t: (i, 0, 0)),
                  _const_spec((1, d)),
                  _const_spec(w_bf.shape),
                  _const_spec(lbl.shape),
                  _const_spec(qg.shape),
                  _const_spec(kg.shape),
                  pl.BlockSpec((tm, LANES), lambda i, t: (t, 0)),
                  pl.BlockSpec((tm, LANES), lambda i, t: (t, 0)),
                  _const_spec(gsum.shape)],
        out_specs=[tok(HG_KEY), tok(HG_KEY), tok(HG_KEY), tok(HG_KEY), tok(HG_KEY), tok(HG_VAL),
                   tok(HG_VAL), tok(ATT_Q), tok(ATT_KV), tok(ATT_KV), tok(d), tok(d)],
        out_shape=[shp(HG_KEY, BF16), shp(HG_KEY, BF16), shp(HG_KEY, F32), shp(HG_KEY, BF16),
                   shp(HG_KEY, F32), shp(HG_VAL, BF16), shp(HG_VAL, BF16), shp(ATT_Q, BF16),
                   shp(ATT_KV, BF16), shp(ATT_KV, BF16), shp(d, BF16), shp(d, BF16)],
        compiler_params=_params(("parallel", "parallel")),
        name="inproj_latent",
    )(x, mods, norm_g, w_bf, lbl, qg, kg, cos, sin, gsum)


def _inproj_context(ctx, mods, ctx_row, norm_g, w_bf, lbl, kg, gsum, layer):
    b, n_ctx, d = ctx.shape
    tm = CTX_TILE
    tok = lambda n: pl.BlockSpec((None, tm, n), lambda i, t: (i, t, 0))
    shp = lambda n, dt: jax.ShapeDtypeStruct((b, n_ctx, n), dt)
    return pl.pallas_call(
        functools.partial(_inproj_context_kernel, layer=layer),
        grid=(b, n_ctx // tm),
        in_specs=[tok(d),
                  pl.BlockSpec((None, N_MOD, d), lambda i, t: (ctx_row, 0, 0)),
                  _const_spec((1, d)),
                  _const_spec(w_bf.shape),
                  _const_spec(lbl.shape),
                  _const_spec(kg.shape),
                  _const_spec(gsum.shape)],
        out_specs=[tok(HG_KEY), tok(HG_KEY), tok(HG_KEY), tok(HG_KEY), tok(HG_VAL),
                   tok(ATT_KV), tok(ATT_KV)],
        out_shape=[shp(HG_KEY, BF16), shp(HG_KEY, F32), shp(HG_KEY, BF16), shp(HG_KEY, F32),
                   shp(HG_VAL, BF16), shp(ATT_KV, BF16), shp(ATT_KV, BF16)],
        compiler_params=_params(("parallel", "parallel")),
        name="inproj_context",
    )(ctx, mods, norm_g, w_bf, lbl, kg, gsum)


_HG_LEVELS = tuple(2 ** i for i in range(int(np.log2(HG_CHUNK))))


def _level_operand(h, ch):
    g, cum, cum_ref, reverse = ch["g"], ch["cum"], ch["cum_ref"], ch["reverse"]
    c = cum.shape[0]
    q32, k32 = ch["q32"], ch["k32"]
    if h >= SUBLANES:
        src, arg = [], []
        for p in range(c // (2 * h)):
            first = slice(p * 2 * h, p * 2 * h + h)
            second = slice(p * 2 * h + h, (p + 1) * 2 * h)
            r = p * 2 * h + (h if reverse else h - 1)
            mid = cum_ref[r:r + 1, :]
            if reverse:
                src += [q32[first], k32[second]]
                arg += [cum[first] - mid, mid - cum[second]]
            else:
                src += [k32[first], q32[second]]
                arg += [mid - cum[first], cum[second] - mid]
        return (jnp.concatenate(src, axis=0) * jnp.exp2(jnp.concatenate(arg, axis=0))).astype(BF16)
    row = lax.broadcasted_iota(I32, (c, 1), 0)
    u = row % (2 * h)
    query = (u < h) if reverse else (u >= h)
    src = jnp.where(query, q32, k32)
    if h == 1:
        arg = jnp.where(query, g, 0.0)
    elif h == 2:
        up = pltpu.roll(g, c - 1, 0)
        down = pltpu.roll(g, 1, 0)
        if reverse:
            arg = jnp.where(u == 0, g + up, jnp.where(u == 1, g, jnp.where(u == 2, 0.0, down)))
        else:
            arg = jnp.where(u == 0, up, jnp.where(u == 1, 0.0, jnp.where(u == 2, g, g + down)))
    else:
        pieces = []
        for p in range(c // (2 * h)):
            r = p * 2 * h + (h if reverse else h - 1)
            pieces.append(jnp.broadcast_to(cum_ref[r:r + 1, :], (2 * h, HG_DK)))
        arg = -jnp.abs(cum - jnp.concatenate(pieces, axis=0))
    return (src * jnp.exp2(arg)).astype(BF16)


def _hgrn_chunks(dirs, tri, level_id, want_o):
    c = HG_CHUNK
    nt = (((1,), (1,)), ((), ()))
    tn = (((0,), (0,)), ((), ()))
    chains = []
    for di, d in enumerate(dirs):
        g = d["g"]
        g_hi = g.astype(BF16)
        g_lo = (g - g_hi.astype(F32)).astype(BF16)
        t = tri[d["reverse"]]
        cum = (jnp.dot(t, g_hi, preferred_element_type=F32)
               + jnp.dot(t, g_lo, preferred_element_type=F32))
        n_heads = g.shape[1] // HG_DK
        for pi in range(n_heads // HG_STACK):
            heads = list(range(pi * HG_STACK, (pi + 1) * HG_STACK))

            def stack(a, heads=heads):
                return jnp.concatenate([a[:, hd * HG_DK:(hd + 1) * HG_DK] for hd in heads], axis=0)

            chains.append(dict(di=di, reverse=d["reverse"], heads=heads, cum=stack(cum),
                               g=stack(g), k=stack(d["k"]), v=stack(d["v"]),
                               q=None if d["q"] is None else stack(d["q"]),
                               st_refs=[d["st_ref"].at[hd] for hd in heads],
                               cum_ref=d["cum_ref"].at[pi]))
    for ch in chains:
        cum = ch["cum"]
        lasts = [cum[i * c:i * c + 1, :] if ch["reverse"] else cum[(i + 1) * c - 1:(i + 1) * c, :]
                 for i in range(HG_STACK)]
        last_rows = jnp.concatenate([jnp.broadcast_to(l, (c, HG_DK)) for l in lasts], axis=0)
        ch["k32"] = ch["k"].astype(F32)
        kl = (ch["k32"] * jnp.exp2(last_rows - cum)).astype(BF16)
        ch["st"] = []
        for i, st_ref in enumerate(ch["st_refs"]):
            rows = slice(i * c, (i + 1) * c)
            st = st_ref[...]
            st_ref[...] = st * jnp.exp2(lasts[i]) + lax.dot_general(
                ch["v"][rows], kl[rows], tn, preferred_element_type=F32)
            ch["st"].append(st)
    if not want_o:
        return None
    for ch in chains:
        ch["q32"] = ch["q"].astype(F32)
        ch["cum_ref"][...] = ch["cum"]
        ch["scores"] = jnp.zeros((HG_STACK * c, HG_STACK * c), F32)
    for li, h in enumerate(_HG_LEVELS):
        for ch in chains:
            m = _level_operand(h, ch)
            s_h = lax.dot_general(m, m, nt, preferred_element_type=F32)
            ch["scores"] = jnp.where(level_id[ch["reverse"]] == li, s_h, ch["scores"])
    outs = [[None] * (len(chains) * HG_STACK // len(dirs)) for _ in dirs]
    for ch in chains:
        self_term = jnp.sum(ch["q32"] * ch["k32"], axis=-1, keepdims=True)
        scores = jnp.where(level_id[ch["reverse"]] == len(_HG_LEVELS), self_term, ch["scores"])
        o = jnp.dot(scores.astype(BF16), ch["v"], preferred_element_type=F32)
        qd = (ch["q32"] * jnp.exp2(ch["cum"])).astype(BF16)
        for i, hd in enumerate(ch["heads"]):
            rows = slice(i * c, (i + 1) * c)
            outs[ch["di"]][hd] = o[rows] + lax.dot_general(
                qd[rows], ch["st"][i].astype(BF16), nt, preferred_element_type=F32)
    return outs


def _hgrn_kernel(q_ref, kf_ref, lf_ref, kb_ref, lb_ref, v_ref,
                 ckf_ref, clf_ref, ckb_ref, clb_ref, cv_ref, gn_ref, o_ref,
                 st_ref, of_ref, ob_ref, cum_ref):
    c = HG_CHUNK
    n_heads = q_ref.shape[1] // HG_DK
    n_ctx = ckf_ref.shape[0] // c
    n_lat = q_ref.shape[0] // c
    row = lax.broadcasted_iota(I32, (c, c), 0)
    col = lax.broadcasted_iota(I32, (c, c), 1)
    tri = {False: jnp.where(col <= row, 1.0, 0.0).astype(BF16),
           True: jnp.where(col >= row, 1.0, 0.0).astype(BF16)}
    rows2 = lax.broadcasted_iota(I32, (HG_STACK * c, HG_STACK * c), 0)
    cols2 = lax.broadcasted_iota(I32, (HG_STACK * c, HG_STACK * c), 1)
    x = rows2 ^ cols2
    lvl = jnp.where(x == 0, len(_HG_LEVELS), -1)
    for li, h in enumerate(_HG_LEVELS):
        lvl = jnp.where((x >= h) & (x < 2 * h), li, lvl)
    level_id = {False: jnp.where(cols2 <= rows2, lvl, -1), True: jnp.where(cols2 >= rows2, lvl, -1)}

    st_ref[...] = jnp.zeros_like(st_ref)

    def dirs_at(a, z, k_f, l_f, k_b, l_b, v, q):
        out = []
        for slot, (rev, start, kk, ll) in enumerate(((False, a, k_f, l_f), (True, z, k_b, l_b))):
            rows = pl.ds(start, c)
            out.append(dict(reverse=rev, g=ll[rows, :], k=kk[rows, :], v=v[rows, :],
                            q=None if q is None else q[rows, :],
                            st_ref=st_ref.at[slot], cum_ref=cum_ref.at[slot]))
        return out

    def ctx_body(j, carry):
        a = pl.multiple_of(j * c, c)
        z = pl.multiple_of((n_ctx - 1 - j) * c, c)
        _hgrn_chunks(dirs_at(a, z, ckf_ref, clf_ref, ckb_ref, clb_ref, cv_ref, None),
                     tri, level_id, False)
        return carry

    lax.fori_loop(0, n_ctx, ctx_body, 0)

    def lat_body(j, carry):
        a = pl.multiple_of(j * c, c)
        z = pl.multiple_of((n_lat - 1 - j) * c, c)
        outs = _hgrn_chunks(dirs_at(a, z, kf_ref, lf_ref, kb_ref, lb_ref, v_ref, q_ref),
                            tri, level_id, True)
        for hd in range(n_heads):
            lanes = slice(hd * HG_DV, (hd + 1) * HG_DV)
            of_ref[pl.ds(a, c), lanes] = outs[0][hd]
            ob_ref[pl.ds(z, c), lanes] = outs[1][hd]
        return carry

    lax.fori_loop(0, n_lat, lat_body, 0)

    for hd in range(n_heads):
        lanes = slice(hd * HG_DV, (hd + 1) * HG_DV)
        o = of_ref[:, lanes] + ob_ref[:, lanes]
        ms = jnp.mean(o * o, axis=-1, keepdims=True)
        o_ref[:, lanes] = (o * lax.rsqrt(ms + NORM_EPS) * gn_ref[:, lanes]).astype(BF16)


def _hgrn(hq, kf, lf, kb, lb, vi, ckf, clf, ckb, clb, cvi, gn):
    b, seq, _ = hq.shape
    n_ctx = ckf.shape[1]
    nh = HG_HEADS_PER_STEP
    lat = pl.BlockSpec((None, seq, nh * HG_DK), lambda i, h: (i, 0, h))
    cx = pl.BlockSpec((None, n_ctx, nh * HG_DK), lambda i, h: (i, 0, h))
    return pl.pallas_call(
        _hgrn_kernel,
        grid=(b, HG_HEADS // nh),
        in_specs=[lat, lat, lat, lat, lat, lat, cx, cx, cx, cx, cx,
                  pl.BlockSpec((1, nh * HG_DV), lambda i, h: (0, h))],
        out_specs=lat,
        out_shape=jax.ShapeDtypeStruct((b, seq, HG_VAL), BF16),
        scratch_shapes=[pltpu.VMEM((2, nh, HG_DV, HG_DK), F32),
                        pltpu.VMEM((seq, nh * HG_DV), F32), pltpu.VMEM((seq, nh * HG_DV), F32),
                        pltpu.VMEM((2, nh // HG_STACK, HG_STACK * HG_CHUNK, HG_DK), F32)],
        compiler_params=_params(("parallel", "parallel")),
        name="hgrn_scan",
    )(hq, kf, lf, kb, lb, vi, ckf, clf, ckb, clb, cvi, gn)


_HEADS_PER_TILE = LANES // ATT_HEAD_DIM


def _attn_kernel(q_ref, kx_ref, vx_ref, kc_ref, vc_ref, o_ref, kp_ref, vp_ref):
    kvh = pl.program_id(1)

    @pl.when(pl.program_id(2) == 0)
    def _build():
        k_all = jnp.concatenate([kc_ref[...], kx_ref[...]], axis=0)
        v_all = jnp.concatenate([vc_ref[...], vx_ref[...]], axis=0)
        r = lax.broadcasted_iota(I32, (ATT_KV, LANES), 0)
        cidx = lax.broadcasted_iota(I32, (ATT_KV, LANES), 1)
        for u in range(_HEADS_PER_TILE):
            place = (cidx // ATT_HEAD_DIM == u) & (r == kvh * ATT_HEAD_DIM + cidx - u * ATT_HEAD_DIM)
            rep = jnp.where(place, 1.0, 0.0).astype(BF16)
            kp_ref[u] = jnp.dot(k_all, rep, preferred_element_type=F32).astype(BF16)
            vp_ref[u] = jnp.dot(v_all, rep, preferred_element_type=F32).astype(BF16)

    for t in range(q_ref.shape[0] // Q_TILE):
        rows = slice(t * Q_TILE, (t + 1) * Q_TILE)
        for pair in range(ATT_GROUPS // _HEADS_PER_TILE):
            lanes = slice(pair * LANES, (pair + 1) * LANES)
            q = q_ref[rows, lanes]
            acc = jnp.zeros(q.shape, F32)
            for u in range(_HEADS_PER_TILE):
                s = lax.dot_general(q, kp_ref[u], (((1,), (1,)), ((), ())),
                                    preferred_element_type=F32)
                p = jnp.exp2(s - jnp.max(s, axis=-1, keepdims=True))
                inv = 1.0 / jnp.sum(p, axis=-1, keepdims=True)
                acc = acc + jnp.dot(p.astype(BF16), vp_ref[u], preferred_element_type=F32) * inv
            o_ref[rows, lanes] = acc.astype(BF16)


def _attention(aq, ak, av, cak, cav):
    b, seq, _ = aq.shape
    n_ctx = cak.shape[1]
    width = ATT_GROUPS * ATT_HEAD_DIM
    kx = pl.BlockSpec((None, seq, ATT_KV), lambda i, h, t: (i, 0, 0))
    kc = pl.BlockSpec((None, n_ctx, ATT_KV), lambda i, h, t: (i, 0, 0))
    rows = Q_TILE * Q_TILES_PER_STEP
    qo = pl.BlockSpec((None, rows, width), lambda i, h, t: (i, t, h))
    return pl.pallas_call(
        _attn_kernel,
        grid=(b, ATT_KV_HEADS, seq // rows),
        in_specs=[qo, kx, kx, kc, kc],
        out_specs=qo,
        out_shape=jax.ShapeDtypeStruct((b, seq, ATT_Q), BF16),
        scratch_shapes=[pltpu.VMEM((_HEADS_PER_TILE, seq + n_ctx, LANES), BF16),
                        pltpu.VMEM((_HEADS_PER_TILE, seq + n_ctx, LANES), BF16)],
        compiler_params=_params(("parallel", "parallel", "arbitrary")),
        name="attention",
    )(aq, ak, av, cak, cav)


def _merge_kernel(on_ref, sg_ref, oa_ref, ga_ref, gb_ref, x_ref, mod_ref, n2_ref,
                  wa_ref, wb_ref, wo_ref, wr_ref, x1_ref, h2_ref, lg_ref):
    tm = x_ref.shape[0]
    parts = [slice(i * tm // MERGE_SUBTILES, (i + 1) * tm // MERGE_SUBTILES)
             for i in range(MERGE_SUBTILES)]
    a, bb, y = {}, {}, {}
    for i, rows in enumerate(parts):
        a_in = (on_ref[rows, :].astype(F32) * sg_ref[rows, :].astype(F32)).astype(BF16)
        a[i] = jnp.dot(a_in, wa_ref[...], preferred_element_type=F32)
        bb[i] = jnp.dot(oa_ref[rows, :], wb_ref[...], preferred_element_type=F32)
    for i, rows in enumerate(parts):
        m = ga_ref[rows, :].astype(F32) * a[i] + gb_ref[rows, :].astype(F32) * bb[i]
        y[i] = jnp.dot(m.astype(BF16), wo_ref[...], preferred_element_type=F32)
    wr = wr_ref[...]
    for i, rows in enumerate(parts):
        x1 = x_ref[rows, :] + mod_ref[2:3, :] * y[i]
        x1_ref[rows, :] = x1
        h2 = _rms_mod(x1, n2_ref[...], mod_ref[3:4, :], mod_ref[4:5, :])
        h_hi = h2.astype(BF16)
        h_lo = (h2 - h_hi.astype(F32)).astype(BF16)
        h2_ref[rows, :] = h_hi
        r = (jnp.dot(h_hi, wr, preferred_element_type=F32)
             + jnp.dot(h_lo, wr, preferred_element_type=F32))
        lg_ref[rows, :] = r[:, :N_EXPERTS] + r[:, N_EXPERTS:2 * N_EXPERTS]


def _merge(on, sg, oatt, sga, sgb, x, mods, norm2_g, wa, wb, wo, wr):
    b, seq, d = x.shape
    tm = TOKEN_TILE
    tok = lambda n: pl.BlockSpec((None, tm, n), lambda i, t: (i, t, 0))
    return pl.pallas_call(
        _merge_kernel,
        grid=(b, seq // tm),
        in_specs=[tok(HG_VAL), tok(HG_VAL), tok(ATT_Q), tok(d), tok(d), tok(d),
                  pl.BlockSpec((None, N_MOD, d), lambda i, t: (i, 0, 0)),
                  _const_spec((1, d)), _const_spec(wa.shape), _const_spec(wb.shape),
                  _const_spec(wo.shape), _const_spec(wr.shape)],
        out_specs=[tok(d), tok(d), tok(N_EXPERTS)],
        out_shape=[jax.ShapeDtypeStruct((b, seq, d), F32),
                   jax.ShapeDtypeStruct((b, seq, d), BF16),
                   jax.ShapeDtypeStruct((b, seq, N_EXPERTS), F32)],
        compiler_params=_params(("parallel", "parallel")),
        name="merge",
    )(on, sg, oatt, sga, sgb, x, mods, norm2_g, wa, wb, wo, wr)


def _route_kernel(lg_ref, pos_ref, gate_ref, starts_ref, *, cap):
    lg = lg_ref[...]
    n_b, n_exp, length = lg.shape
    e = jnp.exp(lg - jnp.max(lg, axis=1, keepdims=True))
    aff = (e / jnp.sum(e, axis=1, keepdims=True)).reshape(n_b * n_exp, length)
    n_e = n_b * n_exp
    bits = lax.bitcast_convert_type(aff, I32)

    def count(mask):
        return jnp.sum(jnp.where(mask, 1.0, 0.0), axis=1, keepdims=True)

    def value_step(i, t):
        cand = t | lax.shift_left(jnp.int32(1), jnp.asarray(30 - i, I32))
        return jnp.where(count(bits >= cand) >= cap, cand, t)

    thr = lax.fori_loop(0, 31, value_step, jnp.zeros((n_e, 1), I32))
    above = bits > thr
    tied = bits == thr
    need = cap - count(above)
    idx = lax.broadcasted_iota(I32, (1, length), 1)
    n_bits = int(np.log2(length))

    def index_step(i, j):
        cand = j | lax.shift_left(jnp.int32(1), jnp.asarray(n_bits - 1 - i, I32))
        return jnp.where(count(tied & (idx < cand)) < need, cand, j)

    last = lax.fori_loop(0, n_bits, index_step, jnp.zeros((n_e, 1), I32))
    sel = above | (tied & (idx <= last))

    r = lax.broadcasted_iota(I32, (LANES, LANES), 0)
    cc = lax.broadcasted_iota(I32, (LANES, LANES), 1)
    before = jnp.where(r < cc, 1.0, 0.0).astype(BF16)
    sel_b = jnp.where(sel, 1.0, 0.0).astype(BF16)
    offset = jnp.zeros((n_e, 1), F32)
    pieces = []
    groups_per_tile = ROUTE_TILE // LANES
    lane = lax.broadcasted_iota(I32, (1, LANES), 1)
    starts = jnp.zeros((n_e, LANES), F32)
    for gidx in range(length // LANES):
        blk = sel_b[:, gidx * LANES:(gidx + 1) * LANES]
        pieces.append(jnp.dot(blk, before, preferred_element_type=F32) + offset)
        offset = offset + jnp.sum(blk.astype(F32), axis=1, keepdims=True)
        if (gidx + 1) % groups_per_tile == 0:
            starts = jnp.where(lane == (gidx + 1) // groups_per_tile, offset, starts)
    pos = jnp.concatenate(pieces, axis=1)
    pos_ref[...] = jnp.where(sel, pos.astype(I32), -1).reshape(n_b, n_exp, length)
    gate_ref[...] = jnp.where(sel, aff, 0.0).reshape(n_b, n_exp, length)
    starts_ref[...] = starts.astype(I32).reshape(n_b, n_exp, LANES)


def _route(logits_t, cap):
    b, n_e, length = logits_t.shape
    assert length // ROUTE_TILE < LANES and n_e % SUBLANES == 0
    spec = _const_spec((b, n_e, length))
    return pl.pallas_call(
        functools.partial(_route_kernel, cap=cap),
        grid=(1,),
        in_specs=[spec],
        out_specs=[spec, spec, _const_spec((b, n_e, LANES))],
        out_shape=[jax.ShapeDtypeStruct((b, n_e, length), I32),
                   jax.ShapeDtypeStruct((b, n_e, length), F32),
                   jax.ShapeDtypeStruct((b, n_e, LANES), I32)],
        compiler_params=_params(("arbitrary",)),
        name="route",
    )(logits_t)


def _tile_windows(starts_ref, sample, tile, n_e, n_tiles):
    stride = n_tiles + 1
    lows, n_win = [], jnp.int32(0)
    for e in range(n_e):
        at = (sample * n_e + e) * stride + tile
        low = (starts_ref[at] // BF16_ROWS) * BF16_ROWS
        lows.append(low)
        n_win = jnp.maximum(n_win, (starts_ref[at + 1] - low + ROUTE_WINDOW - 1) // ROUTE_WINDOW)
    return lows, n_win


def _window(low, w, cap):
    lo = low + w * ROUTE_WINDOW
    start = pl.multiple_of(jnp.minimum(lo, cap - ROUTE_WINDOW), BF16_ROWS)
    slot = start + lax.broadcasted_iota(I32, (ROUTE_WINDOW, 1), 0)
    return start, slot, slot >= lo


def _gather_kernel(starts_ref, pos_ref, h_ref, xe_ref):
    n_e, cap, d = xe_ref.shape
    n_tiles = h_ref.shape[0] // ROUTE_TILE
    sample = pl.program_id(0)
    xe_ref[...] = jnp.zeros_like(xe_ref)

    def tile_body(j, carry):
        t0 = pl.multiple_of(j * ROUTE_TILE, ROUTE_TILE)
        h_tile = h_ref[pl.ds(t0, ROUTE_TILE), :]
        lows, n_win = _tile_windows(starts_ref, sample, j, n_e, n_tiles)

        def window_body(w, c2):
            pieces, begins = [], []
            for e in range(n_e):
                start, slot, live = _window(lows[e], w, cap)
                hit = (pos_ref[e:e + 1, pl.ds(t0, ROUTE_TILE)] == slot) & live
                pieces.append(jnp.where(hit, 1.0, 0.0).astype(BF16))
                begins.append(start)
            onehot = jnp.concatenate(pieces, axis=0)
            rows = jnp.dot(onehot, h_tile, preferred_element_type=F32).astype(BF16)
            for e in range(n_e):
                dst = xe_ref.at[e, pl.ds(begins[e], ROUTE_WINDOW), :]
                dst[...] = dst[...] + rows[e * ROUTE_WINDOW:(e + 1) * ROUTE_WINDOW]
            return c2

        lax.fori_loop(0, n_win, window_body, 0)
        return carry

    lax.fori_loop(0, n_tiles, tile_body, 0)


def _gather(starts, pos, h2, cap):
    b, n_e, length = pos.shape
    d = h2.shape[-1]
    assert length % ROUTE_TILE == 0 and cap % BF16_ROWS == 0 and cap >= ROUTE_WINDOW
    grid_spec = pltpu.PrefetchScalarGridSpec(
        num_scalar_prefetch=1,
        grid=(b,),
        in_specs=[pl.BlockSpec((None, n_e, length), lambda i, st: (i, 0, 0)),
                  pl.BlockSpec((None, length, d), lambda i, st: (i, 0, 0))],
        out_specs=pl.BlockSpec((None, n_e, cap, d), lambda i, st: (i, 0, 0, 0)))
    return pl.pallas_call(
        _gather_kernel,
        grid_spec=grid_spec,
        out_shape=jax.ShapeDtypeStruct((b, n_e, cap, d), BF16),
        compiler_params=_params(("parallel",)),
        name="gather",
    )(starts, pos, h2)


def _ffn_kernel(xe_ref, wg_ref, wu_ref, wd_ref, ye_ref):
    nb, cap, d = xe_ref.shape
    x = xe_ref[...].reshape(nb * cap, d)
    hg = jnp.dot(x, wg_ref[...].astype(BF16), preferred_element_type=F32)
    hu = jnp.dot(x, wu_ref[...].astype(BF16), preferred_element_type=F32)
    hid = (hg * jax.nn.sigmoid(hg) * hu).astype(BF16)
    ye = jnp.dot(hid, wd_ref[...].astype(BF16), preferred_element_type=F32)
    ye_ref[...] = ye.astype(BF16).reshape(nb, cap, d)


def _ffn(xe, wg, wu, wd):
    b, n_e, cap, d = xe.shape
    ff = wg.shape[-1]
    nb = int(np.gcd(b, FFN_BATCH_GROUP))
    tok = pl.BlockSpec((nb, None, cap, d), lambda e, i: (i, e, 0, 0))
    return pl.pallas_call(
        _ffn_kernel,
        grid=(n_e, b // nb),
        in_specs=[tok,
                  pl.BlockSpec((None, d, ff), lambda e, i: (e, 0, 0)),
                  pl.BlockSpec((None, d, ff), lambda e, i: (e, 0, 0)),
                  pl.BlockSpec((None, ff, d), lambda e, i: (e, 0, 0))],
        out_specs=tok,
        out_shape=jax.ShapeDtypeStruct((b, n_e, cap, d), BF16),
        compiler_params=_params(("parallel", "arbitrary")),
        name="expert_ffn",
    )(xe, wg, wu, wd)


def _combine_kernel(starts_ref, pos_ref, gate_ref, ye_ref, x1_ref, mod_ref, fg_ref, o_ref):
    n_e, cap, d = ye_ref.shape
    tm = x1_ref.shape[0]
    n_tiles = pl.num_programs(1)
    lows, n_win = _tile_windows(starts_ref, pl.program_id(0), pl.program_id(1), n_e, n_tiles)

    def window_body(w, y):
        pieces, rows = [], []
        for e in range(n_e):
            start, slot, live = _window(lows[e], w, cap)
            hit = (pos_ref[e:e + 1, :] == slot) & live
            pieces.append(jnp.where(hit, gate_ref[e:e + 1, :], 0.0).astype(BF16))
            rows.append(ye_ref[e, pl.ds(start, ROUTE_WINDOW), :])
        scatter = jnp.concatenate(pieces, axis=0)
        return y + lax.dot_general(scatter, jnp.concatenate(rows, axis=0),
                                   (((0,), (0,)), ((), ())), preferred_element_type=F32)

    y = lax.fori_loop(0, n_win, window_body, jnp.zeros((tm, d), F32))
    x2 = x1_ref[...] + mod_ref[5:6, :] * y
    ms = jnp.mean(x2 * x2, axis=-1, keepdims=True)
    o_ref[...] = x2 * lax.rsqrt(ms + NORM_EPS) * fg_ref[...]


def _combine(starts, pos, gate, ye, x1, mods, final_g):
    b, seq, d = x1.shape
    n_e, cap = ye.shape[1], ye.shape[2]
    tm = ROUTE_TILE
    tok = lambda n: pl.BlockSpec((None, tm, n), lambda i, t, st: (i, t, 0))
    sel = pl.BlockSpec((None, n_e, tm), lambda i, t, st: (i, 0, t))
    grid_spec = pltpu.PrefetchScalarGridSpec(
        num_scalar_prefetch=1,
        grid=(b, seq // tm),
        in_specs=[sel, sel,
                  pl.BlockSpec((None, n_e, cap, d), lambda i, t, st: (i, 0, 0, 0)),
                  tok(d),
                  pl.BlockSpec((None, N_MOD, d), lambda i, t, st: (i, 0, 0)),
                  pl.BlockSpec((1, d), lambda i, t, st: (0, 0))],
        out_specs=tok(d))
    return pl.pallas_call(
        _combine_kernel,
        grid_spec=grid_spec,
        out_shape=jax.ShapeDtypeStruct((b, seq, d), F32),
        compiler_params=_params(("parallel", "arbitrary")),
        name="combine",
    )(starts, pos, gate, ye, x1, mods, final_g)


def _rope_tables(length):
    rows = length // GRID_W
    row = jnp.repeat(jnp.arange(rows, dtype=F32), GRID_W)
    col = jnp.tile(jnp.arange(GRID_W, dtype=F32), rows)
    inv_freq = ROPE_THETA ** (-jnp.arange(0, ROPE_AXIS_DIM, 2, dtype=F32) / ROPE_AXIS_DIM)
    half = ROPE_AXIS_DIM // 2
    cos_parts, sin_parts = [], []
    for pos in (row, col):
        ang = pos[:, None] * inv_freq
        cos_parts += [jnp.cos(ang), jnp.cos(ang)]
        sin_parts += [-jnp.sin(ang), jnp.sin(ang)]
    cos = jnp.concatenate(cos_parts, axis=-1)
    sin = jnp.concatenate(sin_parts, axis=-1)
    reps = LANES // ATT_HEAD_DIM
    assert half * 4 == ATT_HEAD_DIM
    return jnp.tile(cos, (1, reps)), jnp.tile(sin, (1, reps))


def kernel(x, c, ctx, c_ctx, w_mod, b_mod, norm1_g, norm2_g, w_in, hg_lb_logits, hg_norm_g,
           q_norm_g, k_norm_g, w_branch_a, w_branch_b, w_out, w_router, w_exp_gate, w_exp_up,
           w_exp_down, final_norm_g):
    b, seq, d = x.shape
    depth = w_mod.shape[0]
    assert depth == 1, "context-stream update between layers is not implemented"
    layer = 0
    cap = EC_CAPACITY_FACTOR * seq // N_EXPERTS

    n_rows = -(-(b + 1) // SUBLANES) * SUBLANES
    cond = jnp.zeros((n_rows, d), F32).at[:b].set(c).at[b].set(c_ctx)
    mods = _adaln(cond, w_mod[layer], b_mod[layer]).reshape(n_rows, N_MOD, d)

    cos, sin = _rope_tables(seq)
    gsum = jnp.asarray(np.kron(np.eye(ATT_HEADS), np.ones((ATT_HEAD_DIM, ATT_HEAD_DIM))), BF16)
    qg = jnp.tile(q_norm_g[layer], ATT_HEADS).reshape(1, ATT_Q)
    kg = jnp.tile(k_norm_g[layer], ATT_KV_HEADS).reshape(1, ATT_KV)
    w_bf = w_in[layer].astype(BF16)
    w_ctx = jnp.concatenate([w_bf[:, _OFF_FF:_OFF_G], w_bf[:, _OFF_AK:_OFF_AV + ATT_KV]], axis=1)
    n1 = norm1_g[layer].reshape(1, d)

    hq, kf, lf, kb, lb, vi, sg, aq, ak, av, sga, sgb = _inproj_latent(
        x, mods, n1, w_bf, hg_lb_logits, qg, kg, cos, sin, gsum, layer)
    ckf, clf, ckb, clb, cvi, cak, cav = _inproj_context(
        ctx, mods, b, n1, w_ctx, hg_lb_logits, kg, gsum[:ATT_KV, :ATT_KV], layer)

    on = _hgrn(hq, kf, lf, kb, lb, vi, ckf, clf, ckb, clb, cvi,
               hg_norm_g[layer].reshape(1, HG_VAL))
    oatt = _attention(aq, ak, av, cak, cav)

    wr_hi = w_router[layer].astype(BF16)
    wr_lo = (w_router[layer] - wr_hi.astype(F32)).astype(BF16)
    wr = jnp.zeros((d, LANES), BF16).at[:, :N_EXPERTS].set(wr_hi)
    wr = wr.at[:, N_EXPERTS:2 * N_EXPERTS].set(wr_lo)
    x1, h2, logits = _merge(on, sg, oatt, sga, sgb, x, mods, norm2_g[layer].reshape(1, d),
                            w_branch_a[layer].astype(BF16), w_branch_b[layer].astype(BF16),
                            w_out[layer].astype(BF16), wr)

    pos, gate, starts = _route(jnp.swapaxes(logits, 1, 2), cap)
    starts = starts[:, :, :seq // ROUTE_TILE + 1].reshape(-1)
    xe = _gather(starts, pos, h2, cap)
    ye = _ffn(xe, w_exp_gate[layer], w_exp_up[layer], w_exp_down[layer])
    return _combine(starts, pos, gate, ye, x1, mods, final_norm_g.reshape(1, d))
```

```python
import functools

import numpy as np
import jax
import jax.numpy as jnp
from jax import lax
from jax.experimental import pallas as pl
from jax.experimental.pallas import tpu as pltpu

F32 = jnp.float32
BF16 = jnp.bfloat16
I32 = jnp.int32

GRID_W = 64
HG_HEADS = 4
HG_DK = 128
HG_DV = 128
HG_KEY = HG_HEADS * HG_DK
HG_VAL = HG_HEADS * HG_DV
ATT_HEADS = 8
ATT_KV_HEADS = 2
ATT_HEAD_DIM = 64
ATT_GROUPS = ATT_HEADS // ATT_KV_HEADS
ATT_Q = ATT_HEADS * ATT_HEAD_DIM
ATT_KV = ATT_KV_HEADS * ATT_HEAD_DIM
ROPE_AXIS_DIM = ATT_HEAD_DIM // 2
ROPE_THETA = 10000.0
N_EXPERTS = 16
EC_CAPACITY_FACTOR = 2
N_MOD = 6
NORM_EPS = 1e-6
LOG2_E = float(np.log2(np.e))

LANES = 128
SUBLANES = 8
BF16_ROWS = 16
VMEM_LIMIT_BYTES = 56 * 1024 * 1024

HG_CHUNK = 128
HG_HEADS_PER_STEP = 4
HG_STACK = 1
TOKEN_TILE = 512
CTX_TILE = 256
MERGE_SUBTILES = 2
Q_TILE = 512
Q_TILES_PER_STEP = 2
FFN_BATCH_GROUP = 4
ROUTE_TILE = 256
ROUTE_WINDOW = 64

_OFF_HQ = 0
_OFF_FF = _OFF_HQ + HG_KEY
_OFF_FB = _OFF_FF + HG_KEY
_OFF_I = _OFF_FB + HG_KEY
_OFF_G = _OFF_I + HG_VAL
_OFF_AQ = _OFF_G + HG_VAL
_OFF_AK = _OFF_AQ + ATT_Q
_OFF_AV = _OFF_AK + ATT_KV


def _params(sem):
    return pltpu.CompilerParams(dimension_semantics=sem, vmem_limit_bytes=VMEM_LIMIT_BYTES)


def _const_spec(shape):
    zeros = (0,) * len(shape)
    return pl.BlockSpec(shape, lambda *_: zeros)


def _adaln_kernel(c_ref, w_ref, b_ref, o_ref):
    c = c_ref[...]
    a = c * jax.nn.sigmoid(c)
    o_ref[...] = jnp.dot(a, w_ref[...], preferred_element_type=F32,
                         precision=lax.Precision.HIGHEST) + b_ref[...]


def _adaln(cond, w_mod, b_mod):
    n, d = cond.shape
    ncol = w_mod.shape[1]
    return pl.pallas_call(
        _adaln_kernel,
        grid=(ncol // d,),
        in_specs=[pl.BlockSpec((n, d), lambda j: (0, 0)),
                  pl.BlockSpec((d, d), lambda j: (0, j)),
                  pl.BlockSpec((1, d), lambda j: (0, j))],
        out_specs=pl.BlockSpec((n, d), lambda j: (0, j)),
        out_shape=jax.ShapeDtypeStruct((n, ncol), F32),
        compiler_params=_params(("arbitrary",)),
        name="adaln",
    )(cond, w_mod, b_mod.reshape(1, ncol))


def _rms_mod(x, gain, shift, scale):
    ms = jnp.mean(x * x, axis=-1, keepdims=True)
    h = x * lax.rsqrt(ms + NORM_EPS) * gain
    return h * (1.0 + scale) + shift


def _lower_bound(lbl_ref, direction, layer):
    lg = lbl_ref[direction]
    e = jnp.exp(lg - jnp.max(lg, axis=0, keepdims=True))
    p = e / jnp.sum(e, axis=0, keepdims=True)
    return jnp.sum(p[:layer + 1], axis=0, keepdims=True)


def _hgrn_gate(f_pre, lb):
    s = jax.nn.sigmoid(f_pre)
    return (1.0 - lb) * (1.0 - s), jnp.log2(lb + (1.0 - lb) * s)


def _head_rms(a, gsum, gain):
    ssq = jnp.dot((a * a).astype(BF16), gsum, preferred_element_type=F32)
    return a * lax.rsqrt(ssq * (1.0 / ATT_HEAD_DIM) + NORM_EPS) * gain


def _rope(a, cos, sin):
    lane = lax.broadcasted_iota(I32, (1, LANES), 1)
    first = (lane % ROPE_AXIS_DIM) < (ROPE_AXIS_DIM // 2)
    half = ROPE_AXIS_DIM // 2
    outs = []
    for j in range(a.shape[1] // LANES):
        xg = a[:, j * LANES:(j + 1) * LANES]
        partner = jnp.where(first, pltpu.roll(xg, LANES - half, 1), pltpu.roll(xg, half, 1))
        outs.append(xg * cos + partner * sin)
    return outs[0] if len(outs) == 1 else jnp.concatenate(outs, axis=1)


def _inproj_latent_kernel(x_ref, mod_ref, g_ref, w_ref, lbl_ref, qg_ref, kg_ref, cos_ref, sin_ref,
                          gs_ref, hq_ref, kf_ref, lf_ref, kb_ref, lb_ref, vi_ref, sg_ref,
                          aq_ref, ak_ref, av_ref, ga_ref, gb_ref, *, layer):
    hb = _rms_mod(x_ref[...], g_ref[...], mod_ref[0:1, :], mod_ref[1:2, :]).astype(BF16)

    def proj(lo, n):
        return jnp.dot(hb, w_ref[:, lo:lo + n], preferred_element_type=F32)

    hq_ref[...] = proj(_OFF_HQ, HG_KEY).astype(BF16)
    key, logf = _hgrn_gate(proj(_OFF_FF, HG_KEY), _lower_bound(lbl_ref, 0, layer))
    kf_ref[...] = key.astype(BF16)
    lf_ref[...] = logf
    key, logf = _hgrn_gate(proj(_OFF_FB, HG_KEY), _lower_bound(lbl_ref, 1, layer))
    kb_ref[...] = key.astype(BF16)
    lb_ref[...] = logf
    vi_ref[...] = proj(_OFF_I, HG_VAL).astype(BF16)
    g = proj(_OFF_G, HG_VAL)
    sg_ref[...] = (g * jax.nn.sigmoid(g)).astype(BF16)
    cos = cos_ref[...]
    sin = sin_ref[...]
    aq = _head_rms(proj(_OFF_AQ, ATT_Q), gs_ref[...], qg_ref[...])
    aq_ref[...] = (_rope(aq, cos, sin) * (ATT_HEAD_DIM ** -0.5 * LOG2_E)).astype(BF16)
    ak = _head_rms(proj(_OFF_AK, ATT_KV), gs_ref[0:ATT_KV, 0:ATT_KV], kg_ref[...])
    ak_ref[...] = _rope(ak, cos, sin).astype(BF16)
    av_ref[...] = proj(_OFF_AV, ATT_KV).astype(BF16)
    d = ga_ref.shape[-1]
    ga_ref[...] = jax.nn.sigmoid(proj(_OFF_AV + ATT_KV, d)).astype(BF16)
    gb_ref[...] = jax.nn.sigmoid(proj(_OFF_AV + ATT_KV + d, d)).astype(BF16)


def _inproj_context_kernel(x_ref, mod_ref, g_ref, w_ref, lbl_ref, kg_ref, gs_ref,
                           kf_ref, lf_ref, kb_ref, lb_ref, vi_ref, ak_ref, av_ref, *, layer):
    hb = _rms_mod(x_ref[...], g_ref[...], mod_ref[0:1, :], mod_ref[1:2, :]).astype(BF16)

    def proj(lo, n):
        return jnp.dot(hb, w_ref[:, lo:lo + n], preferred_element_type=F32)

    key, logf = _hgrn_gate(proj(0, HG_KEY), _lower_bound(lbl_ref, 0, layer))
    kf_ref[...] = key.astype(BF16)
    lf_ref[...] = logf
    key, logf = _hgrn_gate(proj(HG_KEY, HG_KEY), _lower_bound(lbl_ref, 1, layer))
    kb_ref[...] = key.astype(BF16)
    lb_ref[...] = logf
    vi_ref[...] = proj(2 * HG_KEY, HG_VAL).astype(BF16)
    ak = _head_rms(proj(2 * HG_KEY + HG_VAL, ATT_KV), gs_ref[...], kg_ref[...])
    ak_ref[...] = ak.astype(BF16)
    av_ref[...] = proj(2 * HG_KEY + HG_VAL + ATT_KV, ATT_KV).astype(BF16)


def _inproj_latent(x, mods, norm_g, w_bf, lbl, qg, kg, cos, sin, gsum, layer):
    b, seq, d = x.shape
    tm = TOKEN_TILE
    tok = lambda n: pl.BlockSpec((None, tm, n), lambda i, t: (i, t, 0))
    shp = lambda n, dt: jax.ShapeDtypeStruct((b, seq, n), dt)
    return pl.pallas_call(
        functools.partial(_inproj_latent_kernel, layer=layer),
        grid=(b, seq // tm),
        in_specs=[tok(d),
                  pl.BlockSpec((None, N_MOD, d), lambda i, t: (i, 0, 0)),
                  _const_spec((1, d)),
                  _const_spec(w_bf.shape),
                  _const_spec(lbl.shape),
                  _const_spec(qg.shape),
                  _const_spec(kg.shape),
                  pl.BlockSpec((tm, LANES), lambda i, t: (t, 0)),
                  pl.BlockSpec((tm, LANES), lambda i, t: (t, 0)),
                  _const_spec(gsum.shape)],
        out_specs=[tok(HG_KEY), tok(HG_KEY), tok(HG_KEY), tok(HG_KEY), tok(HG_KEY), tok(HG_VAL),
                   tok(HG_VAL), tok(ATT_Q), tok(ATT_KV), tok(ATT_KV), tok(d), tok(d)],
        out_shape=[shp(HG_KEY, BF16), shp(HG_KEY, BF16), shp(HG_KEY, F32), shp(HG_KEY, BF16),
                   shp(HG_KEY, F32), shp(HG_VAL, BF16), shp(HG_VAL, BF16), shp(ATT_Q, BF16),
                   shp(ATT_KV, BF16), shp(ATT_KV, BF16), shp(d, BF16), shp(d, BF16)],
        compiler_params=_params(("parallel", "parallel")),
        name="inproj_latent",
    )(x, mods, norm_g, w_bf, lbl, qg, kg, cos, sin, gsum)


def _inproj_context(ctx, mods, ctx_row, norm_g, w_bf, lbl, kg, gsum, layer):
    b, n_ctx, d = ctx.shape
    tm = CTX_TILE
    tok = lambda n: pl.BlockSpec((None, tm, n), lambda i, t: (i, t, 0))
    shp = lambda n, dt: jax.ShapeDtypeStruct((b, n_ctx, n), dt)
    return pl.pallas_call(
        functools.partial(_inproj_context_kernel, layer=layer),
        grid=(b, n_ctx // tm),
        in_specs=[tok(d),
                  pl.BlockSpec((None, N_MOD, d), lambda i, t: (ctx_row, 0, 0)),
                  _const_spec((1, d)),
                  _const_spec(w_bf.shape),
                  _const_spec(lbl.shape),
                  _const_spec(kg.shape),
                  _const_spec(gsum.shape)],
        out_specs=[tok(HG_KEY), tok(HG_KEY), tok(HG_KEY), tok(HG_KEY), tok(HG_VAL),
                   tok(ATT_KV), tok(ATT_KV)],
        out_shape=[shp(HG_KEY, BF16), shp(HG_KEY, F32), shp(HG_KEY, BF16), shp(HG_KEY, F32),
                   shp(HG_VAL, BF16), shp(ATT_KV, BF16), shp(ATT_KV, BF16)],
        compiler_params=_params(("parallel", "parallel")),
        name="inproj_context",
    )(ctx, mods, norm_g, w_bf, lbl, kg, gsum)


_HG_LEVELS = tuple(2 ** i for i in range(int(np.log2(HG_CHUNK))))


def _level_operand(h, ch):
    g, cum, cum_ref, reverse = ch["g"], ch["cum"], ch["cum_ref"], ch["reverse"]
    c = cum.shape[0]
    q32, k32 = ch["q32"], ch["k32"]
    if h >= SUBLANES:
        src, arg = [], []
        for p in range(c // (2 * h)):
            first = slice(p * 2 * h, p * 2 * h + h)
            second = slice(p * 2 * h + h, (p + 1) * 2 * h)
            r = p * 2 * h + (h if reverse else h - 1)
            mid = cum_ref[r:r + 1, :]
            if reverse:
                src += [q32[first], k32[second]]
                arg += [cum[first] - mid, mid - cum[second]]
            else:
                src += [k32[first], q32[second]]
                arg += [mid - cum[first], cum[second] - mid]
        return (jnp.concatenate(src, axis=0) * jnp.exp2(jnp.concatenate(arg, axis=0))).astype(BF16)
    row = lax.broadcasted_iota(I32, (c, 1), 0)
    u = row % (2 * h)
    query = (u < h) if reverse else (u >= h)
    src = jnp.where(query, q32, k32)
    if h == 1:
        arg = jnp.where(query, g, 0.0)
    elif h == 2:
        up = pltpu.roll(g, c - 1, 0)
        down = pltpu.roll(g, 1, 0)
        if reverse:
            arg = jnp.where(u == 0, g + up, jnp.where(u == 1, g, jnp.where(u == 2, 0.0, down)))
        else:
            arg = jnp.where(u == 0, up, jnp.where(u == 1, 0.0, jnp.where(u == 2, g, g + down)))
    else:
        pieces = []
        for p in range(c // (2 * h)):
            r = p * 2 * h + (h if reverse else h - 1)
            pieces.append(jnp.broadcast_to(cum_ref[r:r + 1, :], (2 * h, HG_DK)))
        arg = -jnp.abs(cum - jnp.concatenate(pieces, axis=0))
    return (src * jnp.exp2(arg)).astype(BF16)


def _hgrn_chunks(dirs, tri, level_id, want_o):
    c = HG_CHUNK
    nt = (((1,), (1,)), ((), ()))
    tn = (((0,), (0,)), ((), ()))
    chains = []
    for di, d in enumerate(dirs):
        g = d["g"]
        g_hi = g.astype(BF16)
        g_lo = (g - g_hi.astype(F32)).astype(BF16)
        t = tri[d["reverse"]]
        cum = (jnp.dot(t, g_hi, preferred_element_type=F32)
               + jnp.dot(t, g_lo, preferred_element_type=F32))
        n_heads = g.shape[1] // HG_DK
        for pi in range(n_heads // HG_STACK):
            heads = list(range(pi * HG_STACK, (pi + 1) * HG_STACK))

            def stack(a, heads=heads):
                return jnp.concatenate([a[:, hd * HG_DK:(hd + 1) * HG_DK] for hd in heads], axis=0)

            chains.append(dict(di=di, reverse=d["reverse"], heads=heads, cum=stack(cum),
                               g=stack(g), k=stack(d["k"]), v=stack(d["v"]),
                               q=None if d["q"] is None else stack(d["q"]),
                               st_refs=[d["st_ref"].at[hd] for hd in heads],
                               cum_ref=d["cum_ref"].at[pi]))
    for ch in chains:
        cum = ch["cum"]
        lasts = [cum[i * c:i * c + 1, :] if ch["reverse"] else cum[(i + 1) * c - 1:(i + 1) * c, :]
                 for i in range(HG_STACK)]
        last_rows = jnp.concatenate([jnp.broadcast_to(l, (c, HG_DK)) for l in lasts], axis=0)
        ch["k32"] = ch["k"].astype(F32)
        kl = (ch["k32"] * jnp.exp2(last_rows - cum)).astype(BF16)
        ch["st"] = []
        for i, st_ref in enumerate(ch["st_refs"]):
            rows = slice(i * c, (i + 1) * c)
            st = st_ref[...]
            st_ref[...] = st * jnp.exp2(lasts[i]) + lax.dot_general(
                ch["v"][rows], kl[rows], tn, preferred_element_type=F32)
            ch["st"].append(st)
    if not want_o:
        return None
    for ch in chains:
        ch["q32"] = ch["q"].astype(F32)
        ch["cum_ref"][...] = ch["cum"]
        ch["scores"] = jnp.zeros((HG_STACK * c, HG_STACK * c), F32)
    for li, h in enumerate(_HG_LEVELS):
        for ch in chains:
            m = _level_operand(h, ch)
            s_h = lax.dot_general(m, m, nt, preferred_element_type=F32)
            ch["scores"] = jnp.where(level_id[ch["reverse"]] == li, s_h, ch["scores"])
    outs = [[None] * (len(chains) * HG_STACK // len(dirs)) for _ in dirs]
    for ch in chains:
        self_term = jnp.sum(ch["q32"] * ch["k32"], axis=-1, keepdims=True)
        scores = jnp.where(level_id[ch["reverse"]] == len(_HG_LEVELS), self_term, ch["scores"])
        o = jnp.dot(scores.astype(BF16), ch["v"], preferred_element_type=F32)
        qd = (ch["q32"] * jnp.exp2(ch["cum"])).astype(BF16)
        for i, hd in enumerate(ch["heads"]):
            rows = slice(i * c, (i + 1) * c)
            outs[ch["di"]][hd] = o[rows] + lax.dot_general(
                qd[rows], ch["st"][i].astype(BF16), nt, preferred_element_type=F32)
    return outs


def _hgrn_kernel(q_ref, kf_ref, lf_ref, kb_ref, lb_ref, v_ref,
                 ckf_ref, clf_ref, ckb_ref, clb_ref, cv_ref, gn_ref, o_ref,
                 st_ref, of_ref, ob_ref, cum_ref):
    c = HG_CHUNK
    n_heads = q_ref.shape[1] // HG_DK
    n_ctx = ckf_ref.shape[0] // c
    n_lat = q_ref.shape[0] // c
    row = lax.broadcasted_iota(I32, (c, c), 0)
    col = lax.broadcasted_iota(I32, (c, c), 1)
    tri = {False: jnp.where(col <= row, 1.0, 0.0).astype(BF16),
           True: jnp.where(col >= row, 1.0, 0.0).astype(BF16)}
    rows2 = lax.broadcasted_iota(I32, (HG_STACK * c, HG_STACK * c), 0)
    cols2 = lax.broadcasted_iota(I32, (HG_STACK * c, HG_STACK * c), 1)
    x = rows2 ^ cols2
    lvl = jnp.where(x == 0, len(_HG_LEVELS), -1)
    for li, h in enumerate(_HG_LEVELS):
        lvl = jnp.where((x >= h) & (x < 2 * h), li, lvl)
    level_id = {False: jnp.where(cols2 <= rows2, lvl, -1), True: jnp.where(cols2 >= rows2, lvl, -1)}

    st_ref[...] = jnp.zeros_like(st_ref)

    def dirs_at(a, z, k_f, l_f, k_b, l_b, v, q):
        out = []
        for slot, (rev, start, kk, ll) in enumerate(((False, a, k_f, l_f), (True, z, k_b, l_b))):
            rows = pl.ds(start, c)
            out.append(dict(reverse=rev, g=ll[rows, :], k=kk[rows, :], v=v[rows, :],
                            q=None if q is None else q[rows, :],
                            st_ref=st_ref.at[slot], cum_ref=cum_ref.at[slot]))
        return out

    def ctx_body(j, carry):
        a = pl.multiple_of(j * c, c)
        z = pl.multiple_of((n_ctx - 1 - j) * c, c)
        _hgrn_chunks(dirs_at(a, z, ckf_ref, clf_ref, ckb_ref, clb_ref, cv_ref, None),
                     tri, level_id, False)
        return carry

    lax.fori_loop(0, n_ctx, ctx_body, 0)

    def lat_body(j, carry):
        a = pl.multiple_of(j * c, c)
        z = pl.multiple_of((n_lat - 1 - j) * c, c)
        outs = _hgrn_chunks(dirs_at(a, z, kf_ref, lf_ref, kb_ref, lb_ref, v_ref, q_ref),
                            tri, level_id, True)
        for hd in range(n_heads):
            lanes = slice(hd * HG_DV, (hd + 1) * HG_DV)
            of_ref[pl.ds(a, c), lanes] = outs[0][hd]
            ob_ref[pl.ds(z, c), lanes] = outs[1][hd]
        return carry

    lax.fori_loop(0, n_lat, lat_body, 0)

    for hd in range(n_heads):
        lanes = slice(hd * HG_DV, (hd + 1) * HG_DV)
        o = of_ref[:, lanes] + ob_ref[:, lanes]
        ms = jnp.mean(o * o, axis=-1, keepdims=True)
        o_ref[:, lanes] = (o * lax.rsqrt(ms + NORM_EPS) * gn_ref[:, lanes]).astype(BF16)


def _hgrn(hq, kf, lf, kb, lb, vi, ckf, clf, ckb, clb, cvi, gn):
    b, seq, _ = hq.shape
    n_ctx = ckf.shape[1]
    nh = HG_HEADS_PER_STEP
    lat = pl.BlockSpec((None, seq, nh * HG_DK), lambda i, h: (i, 0, h))
    cx = pl.BlockSpec((None, n_ctx, nh * HG_DK), lambda i, h: (i, 0, h))
    return pl.pallas_call(
        _hgrn_kernel,
        grid=(b, HG_HEADS // nh),
        in_specs=[lat, lat, lat, lat, lat, lat, cx, cx, cx, cx, cx,
                  pl.BlockSpec((1, nh * HG_DV), lambda i, h: (0, h))],
        out_specs=lat,
        out_shape=jax.ShapeDtypeStruct((b, seq, HG_VAL), BF16),
        scratch_shapes=[pltpu.VMEM((2, nh, HG_DV, HG_DK), F32),
                        pltpu.VMEM((seq, nh * HG_DV), F32), pltpu.VMEM((seq, nh * HG_DV), F32),
                        pltpu.VMEM((2, nh // HG_STACK, HG_STACK * HG_CHUNK, HG_DK), F32)],
        compiler_params=_params(("parallel", "parallel")),
        name="hgrn_scan",
    )(hq, kf, lf, kb, lb, vi, ckf, clf, ckb, clb, cvi, gn)


_HEADS_PER_TILE = LANES // ATT_HEAD_DIM


def _attn_kernel(q_ref, kx_ref, vx_ref, kc_ref, vc_ref, o_ref, kp_ref, vp_ref):
    kvh = pl.program_id(1)

    @pl.when(pl.program_id(2) == 0)
    def _build():
        k_all = jnp.concatenate([kc_ref[...], kx_ref[...]], axis=0)
        v_all = jnp.concatenate([vc_ref[...], vx_ref[...]], axis=0)
        r = lax.broadcasted_iota(I32, (ATT_KV, LANES), 0)
        cidx = lax.broadcasted_iota(I32, (ATT_KV, LANES), 1)
        for u in range(_HEADS_PER_TILE):
            place = (cidx // ATT_HEAD_DIM == u) & (r == kvh * ATT_HEAD_DIM + cidx - u * ATT_HEAD_DIM)
            rep = jnp.where(place, 1.0, 0.0).astype(BF16)
            kp_ref[u] = jnp.dot(k_all, rep, preferred_element_type=F32).astype(BF16)
            vp_ref[u] = jnp.dot(v_all, rep, preferred_element_type=F32).astype(BF16)

    for t in range(q_ref.shape[0] // Q_TILE):
        rows = slice(t * Q_TILE, (t + 1) * Q_TILE)
        for pair in range(ATT_GROUPS // _HEADS_PER_TILE):
            lanes = slice(pair * LANES, (pair + 1) * LANES)
            q = q_ref[rows, lanes]
            acc = jnp.zeros(q.shape, F32)
            for u in range(_HEADS_PER_TILE):
                s = lax.dot_general(q, kp_ref[u], (((1,), (1,)), ((), ())),
                                    preferred_element_type=F32)
                p = jnp.exp2(s - jnp.max(s, axis=-1, keepdims=True))
                inv = 1.0 / jnp.sum(p, axis=-1, keepdims=True)
                acc = acc + jnp.dot(p.astype(BF16), vp_ref[u], preferred_element_type=F32) * inv
            o_ref[rows, lanes] = acc.astype(BF16)


def _attention(aq, ak, av, cak, cav):
    b, seq, _ = aq.shape
    n_ctx = cak.shape[1]
    width = ATT_GROUPS * ATT_HEAD_DIM
    kx = pl.BlockSpec((None, seq, ATT_KV), lambda i, h, t: (i, 0, 0))
    kc = pl.BlockSpec((None, n_ctx, ATT_KV), lambda i, h, t: (i, 0, 0))
    rows = Q_TILE * Q_TILES_PER_STEP
    qo = pl.BlockSpec((None, rows, width), lambda i, h, t: (i, t, h))
    return pl.pallas_call(
        _attn_kernel,
        grid=(b, ATT_KV_HEADS, seq // rows),
        in_specs=[qo, kx, kx, kc, kc],
        out_specs=qo,
        out_shape=jax.ShapeDtypeStruct((b, seq, ATT_Q), BF16),
        scratch_shapes=[pltpu.VMEM((_HEADS_PER_TILE, seq + n_ctx, LANES), BF16),
                        pltpu.VMEM((_HEADS_PER_TILE, seq + n_ctx, LANES), BF16)],
        compiler_params=_params(("parallel", "parallel", "arbitrary")),
        name="attention",
    )(aq, ak, av, cak, cav)


def _merge_kernel(on_ref, sg_ref, oa_ref, ga_ref, gb_ref, x_ref, mod_ref, n2_ref,
                  wa_ref, wb_ref, wo_ref, wr_ref, x1_ref, h2_ref, lg_ref):
    tm = x_ref.shape[0]
    parts = [slice(i * tm // MERGE_SUBTILES, (i + 1) * tm // MERGE_SUBTILES)
             for i in range(MERGE_SUBTILES)]
    a, bb, y = {}, {}, {}
    for i, rows in enumerate(parts):
        a_in = (on_ref[rows, :].astype(F32) * sg_ref[rows, :].astype(F32)).astype(BF16)
        a[i] = jnp.dot(a_in, wa_ref[...], preferred_element_type=F32)
        bb[i] = jnp.dot(oa_ref[rows, :], wb_ref[...], preferred_element_type=F32)
    for i, rows in enumerate(parts):
        m = ga_ref[rows, :].astype(F32) * a[i] + gb_ref[rows, :].astype(F32) * bb[i]
        y[i] = jnp.dot(m.astype(BF16), wo_ref[...], preferred_element_type=F32)
    wr = wr_ref[...]
    for i, rows in enumerate(parts):
        x1 = x_ref[rows, :] + mod_ref[2:3, :] * y[i]
        x1_ref[rows, :] = x1
        h2 = _rms_mod(x1, n2_ref[...], mod_ref[3:4, :], mod_ref[4:5, :])
        h_hi = h2.astype(BF16)
        h_lo = (h2 - h_hi.astype(F32)).astype(BF16)
        h2_ref[rows, :] = h_hi
        r = (jnp.dot(h_hi, wr, preferred_element_type=F32)
             + jnp.dot(h_lo, wr, preferred_element_type=F32))
        lg_ref[rows, :] = r[:, :N_EXPERTS] + r[:, N_EXPERTS:2 * N_EXPERTS]


def _merge(on, sg, oatt, sga, sgb, x, mods, norm2_g, wa, wb, wo, wr):
    b, seq, d = x.shape
    tm = TOKEN_TILE
    tok = lambda n: pl.BlockSpec((None, tm, n), lambda i, t: (i, t, 0))
    return pl.pallas_call(
        _merge_kernel,
        grid=(b, seq // tm),
        in_specs=[tok(HG_VAL), tok(HG_VAL), tok(ATT_Q), tok(d), tok(d), tok(d),
                  pl.BlockSpec((None, N_MOD, d), lambda i, t: (i, 0, 0)),
                  _const_spec((1, d)), _const_spec(wa.shape), _const_spec(wb.shape),
                  _const_spec(wo.shape), _const_spec(wr.shape)],
        out_specs=[tok(d), tok(d), tok(N_EXPERTS)],
        out_shape=[jax.ShapeDtypeStruct((b, seq, d), F32),
                   jax.ShapeDtypeStruct((b, seq, d), BF16),
                   jax.ShapeDtypeStruct((b, seq, N_EXPERTS), F32)],
        compiler_params=_params(("parallel", "parallel")),
        name="merge",
    )(on, sg, oatt, sga, sgb, x, mods, norm2_g, wa, wb, wo, wr)


def _route_kernel(lg_ref, pos_ref, gate_ref, starts_ref, *, cap):
    lg = lg_ref[...]
    n_b, n_exp, length = lg.shape
    e = jnp.exp(lg - jnp.max(lg, axis=1, keepdims=True))
    aff = (e / jnp.sum(e, axis=1, keepdims=True)).reshape(n_b * n_exp, length)
    n_e = n_b * n_exp
    def count(mask):
        return jnp.sum(jnp.where(mask, 1.0, 0.0), axis=1, keepdims=True)

    def value_step(i, t):
        cand = t | lax.shift_left(jnp.int32(1), jnp.asarray(30 - i, I32))
        enough = count(aff >= lax.bitcast_convert_type(cand, F32)) >= cap
        return jnp.where(enough, cand, t)

    thr = lax.bitcast_convert_type(
        lax.fori_loop(0, 31, value_step, jnp.zeros((n_e, 1), I32)), F32)
    above = aff > thr
    tied = aff == thr
    need = cap - count(above)
    idx = lax.broadcasted_iota(I32, (1, length), 1)
    n_bits = int(np.log2(length))

    def index_step(i, j):
        cand = j | lax.shift_left(jnp.int32(1), jnp.asarray(n_bits - 1 - i, I32))
        return jnp.where(count(tied & (idx < cand)) < need, cand, j)

    last = lax.fori_loop(0, n_bits, index_step, jnp.zeros((n_e, 1), I32))
    sel = above | (tied & (idx <= last))

    r = lax.broadcasted_iota(I32, (LANES, LANES), 0)
    cc = lax.broadcasted_iota(I32, (LANES, LANES), 1)
    before = jnp.where(r < cc, 1.0, 0.0).astype(BF16)
    sel_b = jnp.where(sel, 1.0, 0.0).astype(BF16)
    offset = jnp.zeros((n_e, 1), F32)
    pieces = []
    groups_per_tile = ROUTE_TILE // LANES
    lane = lax.broadcasted_iota(I32, (1, LANES), 1)
    starts = jnp.zeros((n_e, LANES), F32)
    for gidx in range(length // LANES):
        blk = sel_b[:, gidx * LANES:(gidx + 1) * LANES]
        pieces.append(jnp.dot(blk, before, preferred_element_type=F32) + offset)
        offset = offset + jnp.sum(blk.astype(F32), axis=1, keepdims=True)
        if (gidx + 1) % groups_per_tile == 0:
            starts = jnp.where(lane == (gidx + 1) // groups_per_tile, offset, starts)
    pos = jnp.concatenate(pieces, axis=1)
    pos_ref[...] = jnp.where(sel, pos.astype(I32), -1).reshape(n_b, n_exp, length)
    gate_ref[...] = jnp.where(sel, aff, 0.0).reshape(n_b, n_exp, length)
    starts_ref[...] = starts.astype(I32).reshape(n_b, n_exp, LANES)


def _route(logits_t, cap):
    b, n_e, length = logits_t.shape
    assert length // ROUTE_TILE < LANES and n_e % SUBLANES == 0
    spec = _const_spec((b, n_e, length))
    return pl.pallas_call(
        functools.partial(_route_kernel, cap=cap),
        grid=(1,),
        in_specs=[spec],
        out_specs=[spec, spec, _const_spec((b, n_e, LANES))],
        out_shape=[jax.ShapeDtypeStruct((b, n_e, length), I32),
                   jax.ShapeDtypeStruct((b, n_e, length), F32),
                   jax.ShapeDtypeStruct((b, n_e, LANES), I32)],
        compiler_params=_params(("arbitrary",)),
        name="route",
    )(logits_t)


def _tile_windows(starts_ref, sample, tile, n_e, n_tiles):
    stride = n_tiles + 1
    lows, n_win = [], jnp.int32(0)
    for e in range(n_e):
        at = (sample * n_e + e) * stride + tile
        low = (starts_ref[at] // BF16_ROWS) * BF16_ROWS
        lows.append(low)
        n_win = jnp.maximum(n_win, (starts_ref[at + 1] - low + ROUTE_WINDOW - 1) // ROUTE_WINDOW)
    return lows, n_win


def _window(low, w, cap):
    lo = low + w * ROUTE_WINDOW
    start = pl.multiple_of(jnp.minimum(lo, cap - ROUTE_WINDOW), BF16_ROWS)
    slot = start + lax.broadcasted_iota(I32, (ROUTE_WINDOW, 1), 0)
    return start, slot, slot >= lo


def _gather_kernel(starts_ref, pos_ref, h_ref, xe_ref):
    n_e, cap, d = xe_ref.shape
    n_tiles = h_ref.shape[0] // ROUTE_TILE
    sample = pl.program_id(0)
    xe_ref[...] = jnp.zeros_like(xe_ref)

    def tile_body(j, carry):
        t0 = pl.multiple_of(j * ROUTE_TILE, ROUTE_TILE)
        h_tile = h_ref[pl.ds(t0, ROUTE_TILE), :]
        lows, n_win = _tile_windows(starts_ref, sample, j, n_e, n_tiles)

        def window_body(w, c2):
            pieces, begins = [], []
            for e in range(n_e):
                start, slot, live = _window(lows[e], w, cap)
                hit = (pos_ref[e:e + 1, pl.ds(t0, ROUTE_TILE)] == slot) & live
                pieces.append(jnp.where(hit, 1.0, 0.0).astype(BF16))
                begins.append(start)
            onehot = jnp.concatenate(pieces, axis=0)
            rows = jnp.dot(onehot, h_tile, preferred_element_type=F32).astype(BF16)
            for e in range(n_e):
                dst = xe_ref.at[e, pl.ds(begins[e], ROUTE_WINDOW), :]
                dst[...] = dst[...] + rows[e * ROUTE_WINDOW:(e + 1) * ROUTE_WINDOW]
            return c2

        lax.fori_loop(0, n_win, window_body, 0)
        return carry

    lax.fori_loop(0, n_tiles, tile_body, 0)


def _gather(starts, pos, h2, cap):
    b, n_e, length = pos.shape
    d = h2.shape[-1]
    assert length % ROUTE_TILE == 0 and cap % BF16_ROWS == 0 and cap >= ROUTE_WINDOW
    grid_spec = pltpu.PrefetchScalarGridSpec(
        num_scalar_prefetch=1,
        grid=(b,),
        in_specs=[pl.BlockSpec((None, n_e, length), lambda i, st: (i, 0, 0)),
                  pl.BlockSpec((None, length, d), lambda i, st: (i, 0, 0))],
        out_specs=pl.BlockSpec((None, n_e, cap, d), lambda i, st: (i, 0, 0, 0)))
    return pl.pallas_call(
        _gather_kernel,
        grid_spec=grid_spec,
        out_shape=jax.ShapeDtypeStruct((b, n_e, cap, d), BF16),
        compiler_params=_params(("parallel",)),
        name="gather",
    )(starts, pos, h2)


def _ffn_kernel(xe_ref, wg_ref, wu_ref, wd_ref, ye_ref):
    nb, cap, d = xe_ref.shape
    x = xe_ref[...].reshape(nb * cap, d)
    hg = jnp.dot(x, wg_ref[...].astype(BF16), preferred_element_type=F32)
    hu = jnp.dot(x, wu_ref[...].astype(BF16), preferred_element_type=F32)
    hid = (hg * jax.nn.sigmoid(hg) * hu).astype(BF16)
    ye = jnp.dot(hid, wd_ref[...].astype(BF16), preferred_element_type=F32)
    ye_ref[...] = ye.astype(BF16).reshape(nb, cap, d)


def _ffn(xe, wg, wu, wd):
    b, n_e, cap, d = xe.shape
    ff = wg.shape[-1]
    nb = int(np.gcd(b, FFN_BATCH_GROUP))
    tok = pl.BlockSpec((nb, None, cap, d), lambda e, i: (i, e, 0, 0))
    return pl.pallas_call(
        _ffn_kernel,
        grid=(n_e, b // nb),
        in_specs=[tok,
                  pl.BlockSpec((None, d, ff), lambda e, i: (e, 0, 0)),
                  pl.BlockSpec((None, d, ff), lambda e, i: (e, 0, 0)),
                  pl.BlockSpec((None, ff, d), lambda e, i: (e, 0, 0))],
        out_specs=tok,
        out_shape=jax.ShapeDtypeStruct((b, n_e, cap, d), BF16),
        compiler_params=_params(("parallel", "arbitrary")),
        name="expert_ffn",
    )(xe, wg, wu, wd)


def _combine_kernel(starts_ref, pos_ref, gate_ref, ye_ref, x1_ref, mod_ref, fg_ref, o_ref):
    n_e, cap, d = ye_ref.shape
    tm = x1_ref.shape[0]
    n_tiles = pl.num_programs(1)
    lows, n_win = _tile_windows(starts_ref, pl.program_id(0), pl.program_id(1), n_e, n_tiles)

    def window(w):
        pieces, rows = [], []
        for e in range(n_e):
            start, slot, live = _window(lows[e], w, cap)
            hit = (pos_ref[e:e + 1, :] == slot) & live
            pieces.append(jnp.where(hit, gate_ref[e:e + 1, :], 0.0).astype(BF16))
            rows.append(ye_ref[e, pl.ds(start, ROUTE_WINDOW), :])
        scatter = jnp.concatenate(pieces, axis=0)
        return lax.dot_general(scatter, jnp.concatenate(rows, axis=0),
                               (((0,), (0,)), ((), ())), preferred_element_type=F32)

    y = lax.fori_loop(1, n_win, lambda w, y: y + window(w), window(0))
    x2 = x1_ref[...] + mod_ref[5:6, :] * y
    ms = jnp.mean(x2 * x2, axis=-1, keepdims=True)
    o_ref[...] = x2 * lax.rsqrt(ms + NORM_EPS) * fg_ref[...]


def _combine(starts, pos, gate, ye, x1, mods, final_g):
    b, seq, d = x1.shape
    n_e, cap = ye.shape[1], ye.shape[2]
    tm = ROUTE_TILE
    tok = lambda n: pl.BlockSpec((None, tm, n), lambda i, t, st: (i, t, 0))
    sel = pl.BlockSpec((None, n_e, tm), lambda i, t, st: (i, 0, t))
    grid_spec = pltpu.PrefetchScalarGridSpec(
        num_scalar_prefetch=1,
        grid=(b, seq // tm),
        in_specs=[sel, sel,
                  pl.BlockSpec((None, n_e, cap, d), lambda i, t, st: (i, 0, 0, 0)),
                  tok(d),
                  pl.BlockSpec((None, N_MOD, d), lambda i, t, st: (i, 0, 0)),
                  pl.BlockSpec((1, d), lambda i, t, st: (0, 0))],
        out_specs=tok(d))
    return pl.pallas_call(
        _combine_kernel,
        grid_spec=grid_spec,
        out_shape=jax.ShapeDtypeStruct((b, seq, d), F32),
        compiler_params=_params(("parallel", "arbitrary")),
        name="combine",
    )(starts, pos, gate, ye, x1, mods, final_g)


def _rope_tables(length):
    rows = length // GRID_W
    row = jnp.repeat(jnp.arange(rows, dtype=F32), GRID_W)
    col = jnp.tile(jnp.arange(GRID_W, dtype=F32), rows)
    inv_freq = ROPE_THETA ** (-jnp.arange(0, ROPE_AXIS_DIM, 2, dtype=F32) / ROPE_AXIS_DIM)
    half = ROPE_AXIS_DIM // 2
    cos_parts, sin_parts = [], []
    for pos in (row, col):
        ang = pos[:, None] * inv_freq
        cos_parts += [jnp.cos(ang), jnp.cos(ang)]
        sin_parts += [-jnp.sin(ang), jnp.sin(ang)]
    cos = jnp.concatenate(cos_parts, axis=-1)
    sin = jnp.concatenate(sin_parts, axis=-1)
    reps = LANES // ATT_HEAD_DIM
    assert half * 4 == ATT_HEAD_DIM
    return jnp.tile(cos, (1, reps)), jnp.tile(sin, (1, reps))


def kernel(x, c, ctx, c_ctx, w_mod, b_mod, norm1_g, norm2_g, w_in, hg_lb_logits, hg_norm_g,
           q_norm_g, k_norm_g, w_branch_a, w_branch_b, w_out, w_router, w_exp_gate, w_exp_up,
           w_exp_down, final_norm_g):
    b, seq, d = x.shape
    depth = w_mod.shape[0]
    assert depth == 1, "context-stream update between layers is not implemented"
    layer = 0
    cap = EC_CAPACITY_FACTOR * seq // N_EXPERTS

    n_rows = -(-(b + 1) // SUBLANES) * SUBLANES
    cond = jnp.zeros((n_rows, d), F32).at[:b].set(c).at[b].set(c_ctx)
    mods = _adaln(cond, w_mod[layer], b_mod[layer]).reshape(n_rows, N_MOD, d)

    cos, sin = _rope_tables(seq)
    gsum = jnp.asarray(np.kron(np.eye(ATT_HEADS), np.ones((ATT_HEAD_DIM, ATT_HEAD_DIM))), BF16)
    qg = jnp.tile(q_norm_g[layer], ATT_HEADS).reshape(1, ATT_Q)
    kg = jnp.tile(k_norm_g[layer], ATT_KV_HEADS).reshape(1, ATT_KV)
    w_bf = w_in[layer].astype(BF16)
    w_ctx = jnp.concatenate([w_bf[:, _OFF_FF:_OFF_G], w_bf[:, _OFF_AK:_OFF_AV + ATT_KV]], axis=1)
    n1 = norm1_g[layer].reshape(1, d)

    hq, kf, lf, kb, lb, vi, sg, aq, ak, av, sga, sgb = _inproj_latent(
        x, mods, n1, w_bf, hg_lb_logits, qg, kg, cos, sin, gsum, layer)
    ckf, clf, ckb, clb, cvi, cak, cav = _inproj_context(
        ctx, mods, b, n1, w_ctx, hg_lb_logits, kg, gsum[:ATT_KV, :ATT_KV], layer)

    on = _hgrn(hq, kf, lf, kb, lb, vi, ckf, clf, ckb, clb, cvi,
               hg_norm_g[layer].reshape(1, HG_VAL))
    oatt = _attention(aq, ak, av, cak, cav)

    wr_hi = w_router[layer].astype(BF16)
    wr_lo = (w_router[layer] - wr_hi.astype(F32)).astype(BF16)
    wr = jnp.zeros((d, LANES), BF16).at[:, :N_EXPERTS].set(wr_hi)
    wr = wr.at[:, N_EXPERTS:2 * N_EXPERTS].set(wr_lo)
    x1, h2, logits = _merge(on, sg, oatt, sga, sgb, x, mods, norm2_g[layer].reshape(1, d),
                            w_branch_a[layer].astype(BF16), w_branch_b[layer].astype(BF16),
                            w_out[layer].astype(BF16), wr)

    pos, gate, starts = _route(jnp.swapaxes(logits, 1, 2), cap)
    starts = starts[:, :, :seq // ROUTE_TILE + 1].reshape(-1)
    xe = _gather(starts, pos, h2, cap)
    ye = _ffn(xe, w_exp_gate[layer], w_exp_up[layer], w_exp_down[layer])
    return _combine(starts, pos, gate, ye, x1, mods, final_norm_g.reshape(1, d))
```

```python
import functools

import numpy as np
import jax
import jax.numpy as jnp
from jax import lax
from jax.experimental import pallas as pl
from jax.experimental.pallas import tpu as pltpu

F32 = jnp.float32
BF16 = jnp.bfloat16
I32 = jnp.int32

GRID_W = 64
HG_HEADS = 4
HG_DK = 128
HG_DV = 128
HG_KEY = HG_HEADS * HG_DK
HG_VAL = HG_HEADS * HG_DV
ATT_HEADS = 8
ATT_KV_HEADS = 2
ATT_HEAD_DIM = 64
ATT_GROUPS = ATT_HEADS // ATT_KV_HEADS
ATT_Q = ATT_HEADS * ATT_HEAD_DIM
ATT_KV = ATT_KV_HEADS * ATT_HEAD_DIM
ROPE_AXIS_DIM = ATT_HEAD_DIM // 2
ROPE_THETA = 10000.0
N_EXPERTS = 16
EC_CAPACITY_FACTOR = 2
N_MOD = 6
NORM_EPS = 1e-6
LOG2_E = float(np.log2(np.e))

LANES = 128
SUBLANES = 8
BF16_ROWS = 16
VMEM_LIMIT_BYTES = 56 * 1024 * 1024

HG_CHUNK = 128
HG_HEADS_PER_STEP = 4
HG_STACK = 1
TOKEN_TILE = 512
CTX_TILE = 256
MERGE_SUBTILES = 2
Q_TILE = 512
Q_TILES_PER_STEP = 2
FFN_BATCH_GROUP = 4
ROUTE_TILE = 256
ROUTE_WINDOW = 64

_OFF_HQ = 0
_OFF_FF = _OFF_HQ + HG_KEY
_OFF_FB = _OFF_FF + HG_KEY
_OFF_I = _OFF_FB + HG_KEY
_OFF_G = _OFF_I + HG_VAL
_OFF_AQ = _OFF_G + HG_VAL
_OFF_AK = _OFF_AQ + ATT_Q
_OFF_AV = _OFF_AK + ATT_KV


def _params(sem):
    return pltpu.CompilerParams(dimension_semantics=sem, vmem_limit_bytes=VMEM_LIMIT_BYTES)


def _const_spec(shape):
    zeros = (0,) * len(shape)
    return pl.BlockSpec(shape, lambda *_: zeros)


def _adaln_kernel(c_ref, w_ref, b_ref, o_ref):
    c = c_ref[...]
    a = c * jax.nn.sigmoid(c)
    o_ref[...] = jnp.dot(a, w_ref[...], preferred_element_type=F32,
                         precision=lax.Precision.HIGHEST) + b_ref[...]


def _adaln(cond, w_mod, b_mod):
    n, d = cond.shape
    ncol = w_mod.shape[1]
    return pl.pallas_call(
        _adaln_kernel,
        grid=(ncol // d,),
        in_specs=[pl.BlockSpec((n, d), lambda j: (0, 0)),
                  pl.BlockSpec((d, d), lambda j: (0, j)),
                  pl.BlockSpec((1, d), lambda j: (0, j))],
        out_specs=pl.BlockSpec((n, d), lambda j: (0, j)),
        out_shape=jax.ShapeDtypeStruct((n, ncol), F32),
        compiler_params=_params(("arbitrary",)),
        name="adaln",
    )(cond, w_mod, b_mod.reshape(1, ncol))


def _rms_mod(x, gain, shift, scale):
    ms = jnp.mean(x * x, axis=-1, keepdims=True)
    h = x * lax.rsqrt(ms + NORM_EPS) * gain
    return h * (1.0 + scale) + shift


def _lower_bound(lbl_ref, direction, layer):
    lg = lbl_ref[direction]
    e = jnp.exp(lg - jnp.max(lg, axis=0, keepdims=True))
    p = e / jnp.sum(e, axis=0, keepdims=True)
    return jnp.sum(p[:layer + 1], axis=0, keepdims=True)


def _hgrn_gate(f_pre, lb):
    s = jax.nn.sigmoid(f_pre)
    return (1.0 - lb) * (1.0 - s), jnp.log2(lb + (1.0 - lb) * s)


def _head_rms(a, gsum, gain):
    ssq = jnp.dot((a * a).astype(BF16), gsum, preferred_element_type=F32)
    return a * lax.rsqrt(ssq * (1.0 / ATT_HEAD_DIM) + NORM_EPS) * gain


def _rope(a, cos, sin):
    lane = lax.broadcasted_iota(I32, (1, LANES), 1)
    first = (lane % ROPE_AXIS_DIM) < (ROPE_AXIS_DIM // 2)
    half = ROPE_AXIS_DIM // 2
    outs = []
    for j in range(a.shape[1] // LANES):
        xg = a[:, j * LANES:(j + 1) * LANES]
        partner = jnp.where(first, pltpu.roll(xg, LANES - half, 1), pltpu.roll(xg, half, 1))
        outs.append(xg * cos + partner * sin)
    return outs[0] if len(outs) == 1 else jnp.concatenate(outs, axis=1)


def _inproj_latent_kernel(x_ref, mod_ref, g_ref, w_ref, lbl_ref, qg_ref, kg_ref, cos_ref, sin_ref,
                          gs_ref, hq_ref, kf_ref, lf_ref, kb_ref, lb_ref, vi_ref, sg_ref,
                          aq_ref, ak_ref, av_ref, ga_ref, gb_ref, *, layer):
    hb = _rms_mod(x_ref[...], g_ref[...], mod_ref[0:1, :], mod_ref[1:2, :]).astype(BF16)

    def proj(lo, n):
        return jnp.dot(hb, w_ref[:, lo:lo + n], preferred_element_type=F32)

    aq = proj(_OFF_AQ, ATT_Q)
    ak = proj(_OFF_AK, ATT_KV)
    key, logf = _hgrn_gate(proj(_OFF_FF, HG_KEY), _lower_bound(lbl_ref, 0, layer))
    kf_ref[...] = key.astype(BF16)
    lf_ref[...] = logf
    key, logf = _hgrn_gate(proj(_OFF_FB, HG_KEY), _lower_bound(lbl_ref, 1, layer))
    kb_ref[...] = key.astype(BF16)
    lb_ref[...] = logf
    d = ga_ref.shape[-1]
    ga_ref[...] = jax.nn.sigmoid(proj(_OFF_AV + ATT_KV, d)).astype(BF16)
    gb_ref[...] = jax.nn.sigmoid(proj(_OFF_AV + ATT_KV + d, d)).astype(BF16)
    g = proj(_OFF_G, HG_VAL)
    sg_ref[...] = (g * jax.nn.sigmoid(g)).astype(BF16)
    cos = cos_ref[...]
    sin = sin_ref[...]
    aq = _head_rms(aq, gs_ref[...], qg_ref[...])
    aq_ref[...] = (_rope(aq, cos, sin) * (ATT_HEAD_DIM ** -0.5 * LOG2_E)).astype(BF16)
    ak = _head_rms(ak, gs_ref[0:ATT_KV, 0:ATT_KV], kg_ref[...])
    ak_ref[...] = _rope(ak, cos, sin).astype(BF16)
    av_ref[...] = proj(_OFF_AV, ATT_KV).astype(BF16)
    vi_ref[...] = proj(_OFF_I, HG_VAL).astype(BF16)
    hq_ref[...] = proj(_OFF_HQ, HG_KEY).astype(BF16)


def _inproj_context_kernel(x_ref, mod_ref, g_ref, w_ref, lbl_ref, kg_ref, gs_ref,
                           kf_ref, lf_ref, kb_ref, lb_ref, vi_ref, ak_ref, av_ref, *, layer):
    hb = _rms_mod(x_ref[...], g_ref[...], mod_ref[0:1, :], mod_ref[1:2, :]).astype(BF16)

    def proj(lo, n):
        return jnp.dot(hb, w_ref[:, lo:lo + n], preferred_element_type=F32)

    key, logf = _hgrn_gate(proj(0, HG_KEY), _lower_bound(lbl_ref, 0, layer))
    kf_ref[...] = key.astype(BF16)
    lf_ref[...] = logf
    key, logf = _hgrn_gate(proj(HG_KEY, HG_KEY), _lower_bound(lbl_ref, 1, layer))
    kb_ref[...] = key.astype(BF16)
    lb_ref[...] = logf
    vi_ref[...] = proj(2 * HG_KEY, HG_VAL).astype(BF16)
    ak = _head_rms(proj(2 * HG_KEY + HG_VAL, ATT_KV), gs_ref[...], kg_ref[...])
    ak_ref[...] = ak.astype(BF16)
    av_ref[...] = proj(2 * HG_KEY + HG_VAL + ATT_KV, ATT_KV).astype(BF16)


def _inproj_latent(x, mods, norm_g, w_bf, lbl, qg, kg, cos, sin, gsum, layer):
    b, seq, d = x.shape
    tm = TOKEN_TILE
    tok = lambda n: pl.BlockSpec((None, tm, n), lambda i, t: (i, t, 0))
    shp = lambda n, dt: jax.ShapeDtypeStruct((b, seq, n), dt)
    return pl.pallas_call(
        functools.partial(_inproj_latent_kernel, layer=layer),
        grid=(b, seq // tm),
        in_specs=[tok(d),
                  pl.BlockSpec((None, N_MOD, d), lambda i, t: (i, 0, 0)),
                  _const_spec((1, d)),
                  _const_spec(w_bf.shape),
                  _const_spec(lbl.shape),
                  _const_spec(qg.shape),
                  _const_spec(kg.shape),
                  pl.BlockSpec((tm, LANES), lambda i, t: (t, 0)),
                  pl.BlockSpec((tm, LANES), lambda i, t: (t, 0)),
                  _const_spec(gsum.shape)],
        out_specs=[tok(HG_KEY), tok(HG_KEY), tok(HG_KEY), tok(HG_KEY), tok(HG_KEY), tok(HG_VAL),
                   tok(HG_VAL), tok(ATT_Q), tok(ATT_KV), tok(ATT_KV), tok(d), tok(d)],
        out_shape=[shp(HG_KEY, BF16), shp(HG_KEY, BF16), shp(HG_KEY, F32), shp(HG_KEY, BF16),
                   shp(HG_KEY, F32), shp(HG_VAL, BF16), shp(HG_VAL, BF16), shp(ATT_Q, BF16),
                   shp(ATT_KV, BF16), shp(ATT_KV, BF16), shp(d, BF16), shp(d, BF16)],
        compiler_params=_params(("parallel", "parallel")),
        name="inproj_latent",
    )(x, mods, norm_g, w_bf, lbl, qg, kg, cos, sin, gsum)


def _inproj_context(ctx, mods, ctx_row, norm_g, w_bf, lbl, kg, gsum, layer):
    b, n_ctx, d = ctx.shape
    tm = CTX_TILE
    tok = lambda n: pl.BlockSpec((None, tm, n), lambda i, t: (i, t, 0))
    shp = lambda n, dt: jax.ShapeDtypeStruct((b, n_ctx, n), dt)
    return pl.pallas_call(
        functools.partial(_inproj_context_kernel, layer=layer),
        grid=(b, n_ctx // tm),
        in_specs=[tok(d),
                  pl.BlockSpec((None, N_MOD, d), lambda i, t: (ctx_row, 0, 0)),
                  _const_spec((1, d)),
                  _const_spec(w_bf.shape),
                  _const_spec(lbl.shape),
                  _const_spec(kg.shape),
                  _const_spec(gsum.shape)],
        out_specs=[tok(HG_KEY), tok(HG_KEY), tok(HG_KEY), tok(HG_KEY), tok(HG_VAL),
                   tok(ATT_KV), tok(ATT_KV)],
        out_shape=[shp(HG_KEY, BF16), shp(HG_KEY, F32), shp(HG_KEY, BF16), shp(HG_KEY, F32),
                   shp(HG_VAL, BF16), shp(ATT_KV, BF16), shp(ATT_KV, BF16)],
        compiler_params=_params(("parallel", "parallel")),
        name="inproj_context",
    )(ctx, mods, norm_g, w_bf, lbl, kg, gsum)


_HG_LEVELS = tuple(2 ** i for i in range(int(np.log2(HG_CHUNK))))


def _level_operand(h, ch):
    g, cum, cum_ref, reverse = ch["g"], ch["cum"], ch["cum_ref"], ch["reverse"]
    c = cum.shape[0]
    q32, k32 = ch["q32"], ch["k32"]
    if h >= SUBLANES:
        src, arg = [], []
        for p in range(c // (2 * h)):
            first = slice(p * 2 * h, p * 2 * h + h)
            second = slice(p * 2 * h + h, (p + 1) * 2 * h)
            r = p * 2 * h + (h if reverse else h - 1)
            mid = cum_ref[r:r + 1, :]
            if reverse:
                src += [q32[first], k32[second]]
                arg += [cum[first] - mid, mid - cum[second]]
            else:
                src += [k32[first], q32[second]]
                arg += [mid - cum[first], cum[second] - mid]
        return (jnp.concatenate(src, axis=0) * jnp.exp2(jnp.concatenate(arg, axis=0))).astype(BF16)
    row = lax.broadcasted_iota(I32, (c, 1), 0)
    u = row % (2 * h)
    query = (u < h) if reverse else (u >= h)
    src = jnp.where(query, q32, k32)
    if h == 1:
        arg = jnp.where(query, g, 0.0)
    elif h == 2:
        up = pltpu.roll(g, c - 1, 0)
        down = pltpu.roll(g, 1, 0)
        if reverse:
            arg = jnp.where(u == 0, g + up, jnp.where(u == 1, g, jnp.where(u == 2, 0.0, down)))
        else:
            arg = jnp.where(u == 0, up, jnp.where(u == 1, 0.0, jnp.where(u == 2, g, g + down)))
    else:
        pieces = []
        for p in range(c // (2 * h)):
            r = p * 2 * h + (h if reverse else h - 1)
            pieces.append(jnp.broadcast_to(cum_ref[r:r + 1, :], (2 * h, HG_DK)))
        arg = -jnp.abs(cum - jnp.concatenate(pieces, axis=0))
    return (src * jnp.exp2(arg)).astype(BF16)


def _hgrn_chunks(dirs, tri, level_id, want_o):
    c = HG_CHUNK
    nt = (((1,), (1,)), ((), ()))
    tn = (((0,), (0,)), ((), ()))
    chains = []
    for di, d in enumerate(dirs):
        g = d["g"]
        g_hi = g.astype(BF16)
        g_lo = (g - g_hi.astype(F32)).astype(BF16)
        t = tri[d["reverse"]]
        cum = (jnp.dot(t, g_hi, preferred_element_type=F32)
               + jnp.dot(t, g_lo, preferred_element_type=F32))
        n_heads = g.shape[1] // HG_DK
        for pi in range(n_heads // HG_STACK):
            heads = list(range(pi * HG_STACK, (pi + 1) * HG_STACK))

            def stack(a, heads=heads):
                return jnp.concatenate([a[:, hd * HG_DK:(hd + 1) * HG_DK] for hd in heads], axis=0)

            chains.append(dict(di=di, reverse=d["reverse"], heads=heads, cum=stack(cum),
                               g=stack(g), k=stack(d["k"]), v=stack(d["v"]),
                               q=None if d["q"] is None else stack(d["q"]),
                               st_refs=[d["st_ref"].at[hd] for hd in heads],
                               cum_ref=d["cum_ref"].at[pi]))
    for ch in chains:
        cum = ch["cum"]
        lasts = [cum[i * c:i * c + 1, :] if ch["reverse"] else cum[(i + 1) * c - 1:(i + 1) * c, :]
                 for i in range(HG_STACK)]
        last_rows = jnp.concatenate([jnp.broadcast_to(l, (c, HG_DK)) for l in lasts], axis=0)
        ch["k32"] = ch["k"].astype(F32)
        kl = (ch["k32"] * jnp.exp2(last_rows - cum)).astype(BF16)
        ch["st"] = []
        for i, st_ref in enumerate(ch["st_refs"]):
            rows = slice(i * c, (i + 1) * c)
            st = st_ref[...]
            st_ref[...] = st * jnp.exp2(lasts[i]) + lax.dot_general(
                ch["v"][rows], kl[rows], tn, preferred_element_type=F32)
            ch["st"].append(st)
    if not want_o:
        return None
    for ch in chains:
        ch["q32"] = ch["q"].astype(F32)
        ch["cum_ref"][...] = ch["cum"]
        ch["scores"] = jnp.zeros((HG_STACK * c, HG_STACK * c), F32)
    for li, h in enumerate(_HG_LEVELS):
        for ch in chains:
            m = _level_operand(h, ch)
            s_h = lax.dot_general(m, m, nt, preferred_element_type=F32)
            ch["scores"] = jnp.where(level_id[ch["reverse"]] == li, s_h, ch["scores"])
    outs = [[None] * (len(chains) * HG_STACK // len(dirs)) for _ in dirs]
    for ch in chains:
        self_term = jnp.sum(ch["q32"] * ch["k32"], axis=-1, keepdims=True)
        scores = jnp.where(level_id[ch["reverse"]] == len(_HG_LEVELS), self_term, ch["scores"])
        o = jnp.dot(scores.astype(BF16), ch["v"], preferred_element_type=F32)
        qd = (ch["q32"] * jnp.exp2(ch["cum"])).astype(BF16)
        for i, hd in enumerate(ch["heads"]):
            rows = slice(i * c, (i + 1) * c)
            outs[ch["di"]][hd] = o[rows] + lax.dot_general(
                qd[rows], ch["st"][i].astype(BF16), nt, preferred_element_type=F32)
    return outs


def _hgrn_kernel(q_ref, kf_ref, lf_ref, kb_ref, lb_ref, v_ref,
                 ckf_ref, clf_ref, ckb_ref, clb_ref, cv_ref, gn_ref, o_ref,
                 st_ref, of_ref, ob_ref, cum_ref):
    c = HG_CHUNK
    n_heads = q_ref.shape[1] // HG_DK
    n_ctx = ckf_ref.shape[0] // c
    n_lat = q_ref.shape[0] // c
    row = lax.broadcasted_iota(I32, (c, c), 0)
    col = lax.broadcasted_iota(I32, (c, c), 1)
    tri = {False: jnp.where(col <= row, 1.0, 0.0).astype(BF16),
           True: jnp.where(col >= row, 1.0, 0.0).astype(BF16)}
    rows2 = lax.broadcasted_iota(I32, (HG_STACK * c, HG_STACK * c), 0)
    cols2 = lax.broadcasted_iota(I32, (HG_STACK * c, HG_STACK * c), 1)
    x = rows2 ^ cols2
    lvl = jnp.where(x == 0, len(_HG_LEVELS), -1)
    for li, h in enumerate(_HG_LEVELS):
        lvl = jnp.where((x >= h) & (x < 2 * h), li, lvl)
    level_id = {False: jnp.where(cols2 <= rows2, lvl, -1), True: jnp.where(cols2 >= rows2, lvl, -1)}

    st_ref[...] = jnp.zeros_like(st_ref)

    def dirs_at(a, z, k_f, l_f, k_b, l_b, v, q):
        out = []
        for slot, (rev, start, kk, ll) in enumerate(((False, a, k_f, l_f), (True, z, k_b, l_b))):
            rows = pl.ds(start, c)
            out.append(dict(reverse=rev, g=ll[rows, :], k=kk[rows, :], v=v[rows, :],
                            q=None if q is None else q[rows, :],
                            st_ref=st_ref.at[slot], cum_ref=cum_ref.at[slot]))
        return out

    def ctx_body(j, carry):
        a = pl.multiple_of(j * c, c)
        z = pl.multiple_of((n_ctx - 1 - j) * c, c)
        _hgrn_chunks(dirs_at(a, z, ckf_ref, clf_ref, ckb_ref, clb_ref, cv_ref, None),
                     tri, level_id, False)
        return carry

    lax.fori_loop(0, n_ctx, ctx_body, 0)

    def lat_body(j, carry):
        a = pl.multiple_of(j * c, c)
        z = pl.multiple_of((n_lat - 1 - j) * c, c)
        outs = _hgrn_chunks(dirs_at(a, z, kf_ref, lf_ref, kb_ref, lb_ref, v_ref, q_ref),
                            tri, level_id, True)
        for hd in range(n_heads):
            lanes = slice(hd * HG_DV, (hd + 1) * HG_DV)
            of_ref[pl.ds(a, c), lanes] = outs[0][hd]
            ob_ref[pl.ds(z, c), lanes] = outs[1][hd]
        return carry

    lax.fori_loop(0, n_lat, lat_body, 0)

    for hd in range(n_heads):
        lanes = slice(hd * HG_DV, (hd + 1) * HG_DV)
        o = of_ref[:, lanes] + ob_ref[:, lanes]
        ms = jnp.mean(o * o, axis=-1, keepdims=True)
        o_ref[:, lanes] = (o * lax.rsqrt(ms + NORM_EPS) * gn_ref[:, lanes]).astype(BF16)


def _hgrn(hq, kf, lf, kb, lb, vi, ckf, clf, ckb, clb, cvi, gn):
    b, seq, _ = hq.shape
    n_ctx = ckf.shape[1]
    nh = HG_HEADS_PER_STEP
    lat = pl.BlockSpec((None, seq, nh * HG_DK), lambda i, h: (i, 0, h))
    cx = pl.BlockSpec((None, n_ctx, nh * HG_DK), lambda i, h: (i, 0, h))
    return pl.pallas_call(
        _hgrn_kernel,
        grid=(b, HG_HEADS // nh),
        in_specs=[lat, lat, lat, lat, lat, lat, cx, cx, cx, cx, cx,
                  pl.BlockSpec((1, nh * HG_DV), lambda i, h: (0, h))],
        out_specs=lat,
        out_shape=jax.ShapeDtypeStruct((b, seq, HG_VAL), BF16),
        scratch_shapes=[pltpu.VMEM((2, nh, HG_DV, HG_DK), F32),
                        pltpu.VMEM((seq, nh * HG_DV), F32), pltpu.VMEM((seq, nh * HG_DV), F32),
                        pltpu.VMEM((2, nh // HG_STACK, HG_STACK * HG_CHUNK, HG_DK), F32)],
        compiler_params=_params(("parallel", "parallel")),
        name="hgrn_scan",
    )(hq, kf, lf, kb, lb, vi, ckf, clf, ckb, clb, cvi, gn)


_HEADS_PER_TILE = LANES // ATT_HEAD_DIM


def _attn_kernel(q_ref, kx_ref, vx_ref, kc_ref, vc_ref, o_ref, kp_ref, vp_ref):
    kvh = pl.program_id(1)

    @pl.when(pl.program_id(2) == 0)
    def _build():
        k_all = jnp.concatenate([kc_ref[...], kx_ref[...]], axis=0)
        v_all = jnp.concatenate([vc_ref[...], vx_ref[...]], axis=0)
        r = lax.broadcasted_iota(I32, (ATT_KV, LANES), 0)
        cidx = lax.broadcasted_iota(I32, (ATT_KV, LANES), 1)
        for u in range(_HEADS_PER_TILE):
            place = (cidx // ATT_HEAD_DIM == u) & (r == kvh * ATT_HEAD_DIM + cidx - u * ATT_HEAD_DIM)
            rep = jnp.where(place, 1.0, 0.0).astype(BF16)
            kp_ref[u] = jnp.dot(k_all, rep, preferred_element_type=F32).astype(BF16)
            vp_ref[u] = jnp.dot(v_all, rep, preferred_element_type=F32).astype(BF16)

    units = [(slice(t * Q_TILE, (t + 1) * Q_TILE), slice(pair * LANES, (pair + 1) * LANES), u)
             for t in range(q_ref.shape[0] // Q_TILE)
             for pair in range(ATT_GROUPS // _HEADS_PER_TILE)
             for u in range(_HEADS_PER_TILE)]

    def scores(unit):
        rows, lanes, u = unit
        return lax.dot_general(q_ref[rows, lanes], kp_ref[u], (((1,), (1,)), ((), ())),
                               preferred_element_type=F32)

    s_next = scores(units[0])
    acc = None
    for i, (rows, lanes, u) in enumerate(units):
        s = s_next
        if i + 1 < len(units):
            s_next = scores(units[i + 1])
        p = jnp.exp2(s - jnp.max(s, axis=-1, keepdims=True))
        inv = 1.0 / jnp.sum(p, axis=-1, keepdims=True)
        part = jnp.dot(p.astype(BF16), vp_ref[u], preferred_element_type=F32) * inv
        acc = part if u == 0 else acc + part
        if u == _HEADS_PER_TILE - 1:
            o_ref[rows, lanes] = acc.astype(BF16)


def _attention(aq, ak, av, cak, cav):
    b, seq, _ = aq.shape
    n_ctx = cak.shape[1]
    width = ATT_GROUPS * ATT_HEAD_DIM
    kx = pl.BlockSpec((None, seq, ATT_KV), lambda i, h, t: (i, 0, 0))
    kc = pl.BlockSpec((None, n_ctx, ATT_KV), lambda i, h, t: (i, 0, 0))
    rows = Q_TILE * Q_TILES_PER_STEP
    qo = pl.BlockSpec((None, rows, width), lambda i, h, t: (i, t, h))
    return pl.pallas_call(
        _attn_kernel,
        grid=(b, ATT_KV_HEADS, seq // rows),
        in_specs=[qo, kx, kx, kc, kc],
        out_specs=qo,
        out_shape=jax.ShapeDtypeStruct((b, seq, ATT_Q), BF16),
        scratch_shapes=[pltpu.VMEM((_HEADS_PER_TILE, seq + n_ctx, LANES), BF16),
                        pltpu.VMEM((_HEADS_PER_TILE, seq + n_ctx, LANES), BF16)],
        compiler_params=_params(("parallel", "parallel", "arbitrary")),
        name="attention",
    )(aq, ak, av, cak, cav)


def _merge_kernel(on_ref, sg_ref, oa_ref, ga_ref, gb_ref, x_ref, mod_ref, n2_ref,
                  wa_ref, wb_ref, wo_ref, wr_ref, x1_ref, h2_ref, lg_ref):
    tm = x_ref.shape[0]
    parts = [slice(i * tm // MERGE_SUBTILES, (i + 1) * tm // MERGE_SUBTILES)
             for i in range(MERGE_SUBTILES)]
    a, bb, y = {}, {}, {}
    for i, rows in enumerate(parts):
        a_in = (on_ref[rows, :].astype(F32) * sg_ref[rows, :].astype(F32)).astype(BF16)
        a[i] = jnp.dot(a_in, wa_ref[...], preferred_element_type=F32)
        bb[i] = jnp.dot(oa_ref[rows, :], wb_ref[...], preferred_element_type=F32)
    for i, rows in enumerate(parts):
        m = ga_ref[rows, :].astype(F32) * a[i] + gb_ref[rows, :].astype(F32) * bb[i]
        y[i] = jnp.dot(m.astype(BF16), wo_ref[...], preferred_element_type=F32)
    wr = wr_ref[...]
    for i, rows in enumerate(parts):
        x1 = x_ref[rows, :] + mod_ref[2:3, :] * y[i]
        x1_ref[rows, :] = x1
        h2 = _rms_mod(x1, n2_ref[...], mod_ref[3:4, :], mod_ref[4:5, :])
        h_hi = h2.astype(BF16)
        h_lo = (h2 - h_hi.astype(F32)).astype(BF16)
        h2_ref[rows, :] = h_hi
        r = (jnp.dot(h_hi, wr, preferred_element_type=F32)
             + jnp.dot(h_lo, wr, preferred_element_type=F32))
        lg_ref[rows, :] = r[:, :N_EXPERTS] + r[:, N_EXPERTS:2 * N_EXPERTS]


def _merge(on, sg, oatt, sga, sgb, x, mods, norm2_g, wa, wb, wo, wr):
    b, seq, d = x.shape
    tm = TOKEN_TILE
    tok = lambda n: pl.BlockSpec((None, tm, n), lambda i, t: (i, t, 0))
    return pl.pallas_call(
        _merge_kernel,
        grid=(b, seq // tm),
        in_specs=[tok(HG_VAL), tok(HG_VAL), tok(ATT_Q), tok(d), tok(d), tok(d),
                  pl.BlockSpec((None, N_MOD, d), lambda i, t: (i, 0, 0)),
                  _const_spec((1, d)), _const_spec(wa.shape), _const_spec(wb.shape),
                  _const_spec(wo.shape), _const_spec(wr.shape)],
        out_specs=[tok(d), tok(d), tok(N_EXPERTS)],
        out_shape=[jax.ShapeDtypeStruct((b, seq, d), F32),
                   jax.ShapeDtypeStruct((b, seq, d), BF16),
                   jax.ShapeDtypeStruct((b, seq, N_EXPERTS), F32)],
        compiler_params=_params(("parallel", "parallel")),
        name="merge",
    )(on, sg, oatt, sga, sgb, x, mods, norm2_g, wa, wb, wo, wr)


def _route_kernel(lg_ref, pos_ref, gate_ref, starts_ref, *, cap):
    lg = lg_ref[...]
    n_b, n_exp, length = lg.shape
    e = jnp.exp(lg - jnp.max(lg, axis=1, keepdims=True))
    aff = (e / jnp.sum(e, axis=1, keepdims=True)).reshape(n_b * n_exp, length)
    n_e = n_b * n_exp
    def count(mask):
        return jnp.sum(jnp.where(mask, 1.0, 0.0), axis=1, keepdims=True)

    def value_step(i, t):
        cand = t | lax.shift_left(jnp.int32(1), jnp.asarray(30 - i, I32))
        enough = count(aff >= lax.bitcast_convert_type(cand, F32)) >= cap
        return jnp.where(enough, cand, t)

    thr = lax.bitcast_convert_type(
        lax.fori_loop(0, 31, value_step, jnp.zeros((n_e, 1), I32)), F32)
    above = aff > thr
    tied = aff == thr
    need = cap - count(above)
    idx = lax.broadcasted_iota(I32, (1, length), 1)
    n_bits = int(np.log2(length))

    def index_step(i, j):
        cand = j | lax.shift_left(jnp.int32(1), jnp.asarray(n_bits - 1 - i, I32))
        return jnp.where(count(tied & (idx < cand)) < need, cand, j)

    last = lax.fori_loop(0, n_bits, index_step, jnp.zeros((n_e, 1), I32))
    sel = above | (tied & (idx <= last))

    r = lax.broadcasted_iota(I32, (LANES, LANES), 0)
    cc = lax.broadcasted_iota(I32, (LANES, LANES), 1)
    before = jnp.where(r < cc, 1.0, 0.0).astype(BF16)
    sel_b = jnp.where(sel, 1.0, 0.0).astype(BF16)
    offset = jnp.zeros((n_e, 1), F32)
    pieces = []
    groups_per_tile = ROUTE_TILE // LANES
    lane = lax.broadcasted_iota(I32, (1, LANES), 1)
    starts = jnp.zeros((n_e, LANES), F32)
    for gidx in range(length // LANES):
        blk = sel_b[:, gidx * LANES:(gidx + 1) * LANES]
        pieces.append(jnp.dot(blk, before, preferred_element_type=F32) + offset)
        offset = offset + jnp.sum(blk.astype(F32), axis=1, keepdims=True)
        if (gidx + 1) % groups_per_tile == 0:
            starts = jnp.where(lane == (gidx + 1) // groups_per_tile, offset, starts)
    pos = jnp.concatenate(pieces, axis=1)
    pos_ref[...] = jnp.where(sel, pos.astype(I32), -1).reshape(n_b, n_exp, length)
    gate_ref[...] = jnp.where(sel, aff, 0.0).reshape(n_b, n_exp, length)
    starts_ref[...] = starts.astype(I32).reshape(n_b, n_exp, LANES)


def _route(logits_t, cap):
    b, n_e, length = logits_t.shape
    assert length // ROUTE_TILE < LANES and n_e % SUBLANES == 0
    spec = _const_spec((b, n_e, length))
    return pl.pallas_call(
        functools.partial(_route_kernel, cap=cap),
        grid=(1,),
        in_specs=[spec],
        out_specs=[spec, spec, _const_spec((b, n_e, LANES))],
        out_shape=[jax.ShapeDtypeStruct((b, n_e, length), I32),
                   jax.ShapeDtypeStruct((b, n_e, length), F32),
                   jax.ShapeDtypeStruct((b, n_e, LANES), I32)],
        compiler_params=_params(("arbitrary",)),
        name="route",
    )(logits_t)


def _tile_windows(starts_ref, sample, tile, n_e, n_tiles):
    stride = n_tiles + 1
    lows, n_win = [], jnp.int32(0)
    for e in range(n_e):
        at = (sample * n_e + e) * stride + tile
        low = (starts_ref[at] // BF16_ROWS) * BF16_ROWS
        lows.append(low)
        n_win = jnp.maximum(n_win, (starts_ref[at + 1] - low + ROUTE_WINDOW - 1) // ROUTE_WINDOW)
    return lows, n_win


def _window(low, w, cap):
    lo = low + w * ROUTE_WINDOW
    start = pl.multiple_of(jnp.minimum(lo, cap - ROUTE_WINDOW), BF16_ROWS)
    slot = start + lax.broadcasted_iota(I32, (ROUTE_WINDOW, 1), 0)
    return start, slot, slot >= lo


def _gather_kernel(starts_ref, pos_ref, h_ref, xe_ref):
    n_e, cap, d = xe_ref.shape
    n_tiles = h_ref.shape[0] // ROUTE_TILE
    sample = pl.program_id(0)
    xe_ref[...] = jnp.zeros_like(xe_ref)

    def tile_body(j, carry):
        t0 = pl.multiple_of(j * ROUTE_TILE, ROUTE_TILE)
        h_tile = h_ref[pl.ds(t0, ROUTE_TILE), :]
        lows, n_win = _tile_windows(starts_ref, sample, j, n_e, n_tiles)

        def window_body(w, c2):
            pieces, begins = [], []
            for e in range(n_e):
                start, slot, live = _window(lows[e], w, cap)
                hit = (pos_ref[e:e + 1, pl.ds(t0, ROUTE_TILE)] == slot) & live
                pieces.append(jnp.where(hit, 1.0, 0.0).astype(BF16))
                begins.append(start)
            onehot = jnp.concatenate(pieces, axis=0)
            rows = jnp.dot(onehot, h_tile, preferred_element_type=F32).astype(BF16)
            for e in range(n_e):
                dst = xe_ref.at[e, pl.ds(begins[e], ROUTE_WINDOW), :]
                dst[...] = dst[...] + rows[e * ROUTE_WINDOW:(e + 1) * ROUTE_WINDOW]
            return c2

        lax.fori_loop(0, n_win, window_body, 0)
        return carry

    lax.fori_loop(0, n_tiles, tile_body, 0)


def _gather(starts, pos, h2, cap):
    b, n_e, length = pos.shape
    d = h2.shape[-1]
    assert length % ROUTE_TILE == 0 and cap % BF16_ROWS == 0 and cap >= ROUTE_WINDOW
    grid_spec = pltpu.PrefetchScalarGridSpec(
        num_scalar_prefetch=1,
        grid=(b,),
        in_specs=[pl.BlockSpec((None, n_e, length), lambda i, st: (i, 0, 0)),
                  pl.BlockSpec((None, length, d), lambda i, st: (i, 0, 0))],
        out_specs=pl.BlockSpec((None, n_e, cap, d), lambda i, st: (i, 0, 0, 0)))
    return pl.pallas_call(
        _gather_kernel,
        grid_spec=grid_spec,
        out_shape=jax.ShapeDtypeStruct((b, n_e, cap, d), BF16),
        compiler_params=_params(("parallel",)),
        name="gather",
    )(starts, pos, h2)


def _ffn_kernel(xe_ref, wg_ref, wu_ref, wd_ref, ye_ref):
    nb, cap, d = xe_ref.shape
    x = xe_ref[...].reshape(nb * cap, d)
    hg = jnp.dot(x, wg_ref[...].astype(BF16), preferred_element_type=F32)
    hu = jnp.dot(x, wu_ref[...].astype(BF16), preferred_element_type=F32)
    hid = (hg * jax.nn.sigmoid(hg) * hu).astype(BF16)
    ye = jnp.dot(hid, wd_ref[...].astype(BF16), preferred_element_type=F32)
    ye_ref[...] = ye.astype(BF16).reshape(nb, cap, d)


def _ffn(xe, wg, wu, wd):
    b, n_e, cap, d = xe.shape
    ff = wg.shape[-1]
    nb = int(np.gcd(b, FFN_BATCH_GROUP))
    tok = pl.BlockSpec((nb, None, cap, d), lambda e, i: (i, e, 0, 0))
    return pl.pallas_call(
        _ffn_kernel,
        grid=(n_e, b // nb),
        in_specs=[tok,
                  pl.BlockSpec((None, d, ff), lambda e, i: (e, 0, 0)),
                  pl.BlockSpec((None, d, ff), lambda e, i: (e, 0, 0)),
                  pl.BlockSpec((None, ff, d), lambda e, i: (e, 0, 0))],
        out_specs=tok,
        out_shape=jax.ShapeDtypeStruct((b, n_e, cap, d), BF16),
        compiler_params=_params(("parallel", "arbitrary")),
        name="expert_ffn",
    )(xe, wg, wu, wd)


def _combine_kernel(starts_ref, pos_ref, gate_ref, ye_ref, x1_ref, mod_ref, fg_ref, o_ref):
    n_e, cap, _ = ye_ref.shape
    n_tiles = pl.num_programs(1)
    lows, n_win = _tile_windows(starts_ref, pl.program_id(0), pl.program_id(1), n_e, n_tiles)

    def window(w):
        pieces, rows = [], []
        for e in range(n_e):
            start, slot, live = _window(lows[e], w, cap)
            hit = (pos_ref[e:e + 1, :] == slot) & live
            pieces.append(jnp.where(hit, gate_ref[e:e + 1, :], 0.0).astype(BF16))
            rows.append(ye_ref[e, pl.ds(start, ROUTE_WINDOW), :])
        scatter = jnp.concatenate(pieces, axis=0)
        return lax.dot_general(scatter, jnp.concatenate(rows, axis=0),
                               (((0,), (0,)), ((), ())), preferred_element_type=F32)

    y = lax.fori_loop(1, n_win, lambda w, y: y + window(w), window(0))
    x2 = x1_ref[...] + mod_ref[5:6, :] * y
    ms = jnp.mean(x2 * x2, axis=-1, keepdims=True)
    o_ref[...] = x2 * lax.rsqrt(ms + NORM_EPS) * fg_ref[...]


def _combine(starts, pos, gate, ye, x1, mods, final_g):
    b, seq, d = x1.shape
    n_e, cap = ye.shape[1], ye.shape[2]
    tm = ROUTE_TILE
    tok = lambda n: pl.BlockSpec((None, tm, n), lambda i, t, st: (i, t, 0))
    sel = pl.BlockSpec((None, n_e, tm), lambda i, t, st: (i, 0, t))
    grid_spec = pltpu.PrefetchScalarGridSpec(
        num_scalar_prefetch=1,
        grid=(b, seq // tm),
        in_specs=[sel, sel,
                  pl.BlockSpec((None, n_e, cap, d), lambda i, t, st: (i, 0, 0, 0)),
                  tok(d),
                  pl.BlockSpec((None, N_MOD, d), lambda i, t, st: (i, 0, 0)),
                  pl.BlockSpec((1, d), lambda i, t, st: (0, 0))],
        out_specs=tok(d))
    return pl.pallas_call(
        _combine_kernel,
        grid_spec=grid_spec,
        out_shape=jax.ShapeDtypeStruct((b, seq, d), F32),
        compiler_params=_params(("parallel", "arbitrary")),
        name="combine",
    )(starts, pos, gate, ye, x1, mods, final_g)


def _rope_tables(length):
    rows = length // GRID_W
    row = jnp.repeat(jnp.arange(rows, dtype=F32), GRID_W)
    col = jnp.tile(jnp.arange(GRID_W, dtype=F32), rows)
    inv_freq = ROPE_THETA ** (-jnp.arange(0, ROPE_AXIS_DIM, 2, dtype=F32) / ROPE_AXIS_DIM)
    half = ROPE_AXIS_DIM // 2
    cos_parts, sin_parts = [], []
    for pos in (row, col):
        ang = pos[:, None] * inv_freq
        cos_parts += [jnp.cos(ang), jnp.cos(ang)]
        sin_parts += [-jnp.sin(ang), jnp.sin(ang)]
    cos = jnp.concatenate(cos_parts, axis=-1)
    sin = jnp.concatenate(sin_parts, axis=-1)
    reps = LANES // ATT_HEAD_DIM
    assert half * 4 == ATT_HEAD_DIM
    return jnp.tile(cos, (1, reps)), jnp.tile(sin, (1, reps))


def kernel(x, c, ctx, c_ctx, w_mod, b_mod, norm1_g, norm2_g, w_in, hg_lb_logits, hg_norm_g,
           q_norm_g, k_norm_g, w_branch_a, w_branch_b, w_out, w_router, w_exp_gate, w_exp_up,
           w_exp_down, final_norm_g):
    b, seq, d = x.shape
    depth = w_mod.shape[0]
    assert depth == 1, "context-stream update between layers is not implemented"
    layer = 0
    cap = EC_CAPACITY_FACTOR * seq // N_EXPERTS

    n_rows = -(-(b + 1) // SUBLANES) * SUBLANES
    cond = jnp.zeros((n_rows, d), F32).at[:b].set(c).at[b].set(c_ctx)
    mods = _adaln(cond, w_mod[layer], b_mod[layer]).reshape(n_rows, N_MOD, d)

    cos, sin = _rope_tables(seq)
    gsum = jnp.asarray(np.kron(np.eye(ATT_HEADS), np.ones((ATT_HEAD_DIM, ATT_HEAD_DIM))), BF16)
    qg = jnp.tile(q_norm_g[layer], ATT_HEADS).reshape(1, ATT_Q)
    kg = jnp.tile(k_norm_g[layer], ATT_KV_HEADS).reshape(1, ATT_KV)
    w_bf = w_in[layer].astype(BF16)
    w_ctx = jnp.concatenate([w_bf[:, _OFF_FF:_OFF_G], w_bf[:, _OFF_AK:_OFF_AV + ATT_KV]], axis=1)
    n1 = norm1_g[layer].reshape(1, d)

    hq, kf, lf, kb, lb, vi, sg, aq, ak, av, sga, sgb = _inproj_latent(
        x, mods, n1, w_bf, hg_lb_logits, qg, kg, cos, sin, gsum, layer)
    ckf, clf, ckb, clb, cvi, cak, cav = _inproj_context(
        ctx, mods, b, n1, w_ctx, hg_lb_logits, kg, gsum[:ATT_KV, :ATT_KV], layer)

    on = _hgrn(hq, kf, lf, kb, lb, vi, ckf, clf, ckb, clb, cvi,
               hg_norm_g[layer].reshape(1, HG_VAL))
    oatt = _attention(aq, ak, av, cak, cav)

    wr_hi = w_router[layer].astype(BF16)
    wr_lo = (w_router[layer] - wr_hi.astype(F32)).astype(BF16)
    wr = jnp.zeros((d, LANES), BF16).at[:, :N_EXPERTS].set(wr_hi)
    wr = wr.at[:, N_EXPERTS:2 * N_EXPERTS].set(wr_lo)
    x1, h2, logits = _merge(on, sg, oatt, sga, sgb, x, mods, norm2_g[layer].reshape(1, d),
                            w_branch_a[layer].astype(BF16), w_branch_b[layer].astype(BF16),
                            w_out[layer].astype(BF16), wr)

    pos, gate, starts = _route(jnp.swapaxes(logits, 1, 2), cap)
    starts = starts[:, :, :seq // ROUTE_TILE + 1].reshape(-1)
    xe = _gather(starts, pos, h2, cap)
    ye = _ffn(xe, w_exp_gate[layer], w_exp_up[layer], w_exp_down[layer])
    return _combine(starts, pos, gate, ye, x1, mods, final_norm_g.reshape(1, d))
```

```python
import functools

import numpy as np
import jax
import jax.numpy as jnp
from jax import lax
from jax.experimental import pallas as pl
from jax.experimental.pallas import tpu as pltpu

F32 = jnp.float32
BF16 = jnp.bfloat16
I32 = jnp.int32

GRID_W = 64
HG_HEADS = 4
HG_DK = 128
HG_DV = 128
HG_KEY = HG_HEADS * HG_DK
HG_VAL = HG_HEADS * HG_DV
ATT_HEADS = 8
ATT_KV_HEADS = 2
ATT_HEAD_DIM = 64
ATT_GROUPS = ATT_HEADS // ATT_KV_HEADS
ATT_Q = ATT_HEADS * ATT_HEAD_DIM
ATT_KV = ATT_KV_HEADS * ATT_HEAD_DIM
ROPE_AXIS_DIM = ATT_HEAD_DIM // 2
ROPE_THETA = 10000.0
N_EXPERTS = 16
EC_CAPACITY_FACTOR = 2
N_MOD = 6
NORM_EPS = 1e-6
LOG2_E = float(np.log2(np.e))

LANES = 128
SUBLANES = 8
BF16_ROWS = 16
VMEM_LIMIT_BYTES = 56 * 1024 * 1024

HG_CHUNK = 128
HG_HEADS_PER_STEP = 4
HG_STACK = 1
TOKEN_TILE = 512
CTX_TILE = 256
MERGE_SUBTILES = 2
Q_TILE = 512
Q_TILES_PER_STEP = 2
FFN_BATCH_GROUP = 4
ROUTE_TILE = 256
ROUTE_WINDOW = 64
COMBINE_SUBTILES = 2

_OFF_HQ = 0
_OFF_FF = _OFF_HQ + HG_KEY
_OFF_FB = _OFF_FF + HG_KEY
_OFF_I = _OFF_FB + HG_KEY
_OFF_G = _OFF_I + HG_VAL
_OFF_AQ = _OFF_G + HG_VAL
_OFF_AK = _OFF_AQ + ATT_Q
_OFF_AV = _OFF_AK + ATT_KV


def _params(sem):
    return pltpu.CompilerParams(dimension_semantics=sem, vmem_limit_bytes=VMEM_LIMIT_BYTES)


def _const_spec(shape):
    zeros = (0,) * len(shape)
    return pl.BlockSpec(shape, lambda *_: zeros)


def _adaln_kernel(c_ref, w_ref, b_ref, o_ref):
    c = c_ref[...]
    a = c * jax.nn.sigmoid(c)
    o_ref[...] = jnp.dot(a, w_ref[...], preferred_element_type=F32,
                         precision=lax.Precision.HIGHEST) + b_ref[...]


def _adaln(cond, w_mod, b_mod):
    n, d = cond.shape
    ncol = w_mod.shape[1]
    return pl.pallas_call(
        _adaln_kernel,
        grid=(ncol // d,),
        in_specs=[pl.BlockSpec((n, d), lambda j: (0, 0)),
                  pl.BlockSpec((d, d), lambda j: (0, j)),
                  pl.BlockSpec((1, d), lambda j: (0, j))],
        out_specs=pl.BlockSpec((n, d), lambda j: (0, j)),
        out_shape=jax.ShapeDtypeStruct((n, ncol), F32),
        compiler_params=_params(("arbitrary",)),
        name="adaln",
    )(cond, w_mod, b_mod.reshape(1, ncol))


def _rms_mod(x, gain, shift, scale):
    ms = jnp.mean(x * x, axis=-1, keepdims=True)
    h = x * lax.rsqrt(ms + NORM_EPS) * gain
    return h * (1.0 + scale) + shift


def _lower_bound(lbl_ref, direction, layer):
    lg = lbl_ref[direction]
    e = jnp.exp(lg - jnp.max(lg, axis=0, keepdims=True))
    p = e / jnp.sum(e, axis=0, keepdims=True)
    return jnp.sum(p[:layer + 1], axis=0, keepdims=True)


def _hgrn_gate(f_pre, lb):
    s = jax.nn.sigmoid(f_pre)
    return (1.0 - lb) * (1.0 - s), jnp.log2(lb + (1.0 - lb) * s)


def _head_rms(a, gsum, gain):
    ssq = jnp.dot((a * a).astype(BF16), gsum, preferred_element_type=F32)
    return a * lax.rsqrt(ssq * (1.0 / ATT_HEAD_DIM) + NORM_EPS) * gain


def _rope(a, cos, sin):
    lane = lax.broadcasted_iota(I32, (1, LANES), 1)
    first = (lane % ROPE_AXIS_DIM) < (ROPE_AXIS_DIM // 2)
    half = ROPE_AXIS_DIM // 2
    outs = []
    for j in range(a.shape[1] // LANES):
        xg = a[:, j * LANES:(j + 1) * LANES]
        partner = jnp.where(first, pltpu.roll(xg, LANES - half, 1), pltpu.roll(xg, half, 1))
        outs.append(xg * cos + partner * sin)
    return outs[0] if len(outs) == 1 else jnp.concatenate(outs, axis=1)


def _inproj_latent_kernel(x_ref, mod_ref, g_ref, w_ref, lbl_ref, qg_ref, kg_ref, cos_ref, sin_ref,
                          gs_ref, hq_ref, kf_ref, lf_ref, kb_ref, lb_ref, vi_ref, sg_ref,
                          aq_ref, ak_ref, av_ref, ga_ref, gb_ref, *, layer):
    hb = _rms_mod(x_ref[...], g_ref[...], mod_ref[0:1, :], mod_ref[1:2, :]).astype(BF16)

    def proj(lo, n):
        return jnp.dot(hb, w_ref[:, lo:lo + n], preferred_element_type=F32)

    aq = proj(_OFF_AQ, ATT_Q)
    ak = proj(_OFF_AK, ATT_KV)
    key, logf = _hgrn_gate(proj(_OFF_FF, HG_KEY), _lower_bound(lbl_ref, 0, layer))
    kf_ref[...] = key.astype(BF16)
    lf_ref[...] = logf
    key, logf = _hgrn_gate(proj(_OFF_FB, HG_KEY), _lower_bound(lbl_ref, 1, layer))
    kb_ref[...] = key.astype(BF16)
    lb_ref[...] = logf
    d = ga_ref.shape[-1]
    ga_ref[...] = jax.nn.sigmoid(proj(_OFF_AV + ATT_KV, d)).astype(BF16)
    gb_ref[...] = jax.nn.sigmoid(proj(_OFF_AV + ATT_KV + d, d)).astype(BF16)
    g = proj(_OFF_G, HG_VAL)
    sg_ref[...] = (g * jax.nn.sigmoid(g)).astype(BF16)
    cos = cos_ref[...]
    sin = sin_ref[...]
    aq = _head_rms(aq, gs_ref[...], qg_ref[...])
    aq_ref[...] = (_rope(aq, cos, sin) * (ATT_HEAD_DIM ** -0.5 * LOG2_E)).astype(BF16)
    ak = _head_rms(ak, gs_ref[0:ATT_KV, 0:ATT_KV], kg_ref[...])
    ak_ref[...] = _rope(ak, cos, sin).astype(BF16)
    av_ref[...] = proj(_OFF_AV, ATT_KV).astype(BF16)
    vi_ref[...] = proj(_OFF_I, HG_VAL).astype(BF16)
    hq_ref[...] = proj(_OFF_HQ, HG_KEY).astype(BF16)


def _inproj_context_kernel(x_ref, mod_ref, g_ref, w_ref, lbl_ref, kg_ref, gs_ref,
                           kf_ref, lf_ref, kb_ref, lb_ref, vi_ref, ak_ref, av_ref, *, layer):
    hb = _rms_mod(x_ref[...], g_ref[...], mod_ref[0:1, :], mod_ref[1:2, :]).astype(BF16)

    def proj(lo, n):
        return jnp.dot(hb, w_ref[:, lo:lo + n], preferred_element_type=F32)

    key, logf = _hgrn_gate(proj(0, HG_KEY), _lower_bound(lbl_ref, 0, layer))
    kf_ref[...] = key.astype(BF16)
    lf_ref[...] = logf
    key, logf = _hgrn_gate(proj(HG_KEY, HG_KEY), _lower_bound(lbl_ref, 1, layer))
    kb_ref[...] = key.astype(BF16)
    lb_ref[...] = logf
    vi_ref[...] = proj(2 * HG_KEY, HG_VAL).astype(BF16)
    ak = _head_rms(proj(2 * HG_KEY + HG_VAL, ATT_KV), gs_ref[...], kg_ref[...])
    ak_ref[...] = ak.astype(BF16)
    av_ref[...] = proj(2 * HG_KEY + HG_VAL + ATT_KV, ATT_KV).astype(BF16)


def _inproj_latent(x, mods, norm_g, w_bf, lbl, qg, kg, cos, sin, gsum, layer):
    b, seq, d = x.shape
    tm = TOKEN_TILE
    tok = lambda n: pl.BlockSpec((None, tm, n), lambda i, t: (i, t, 0))
    shp = lambda n, dt: jax.ShapeDtypeStruct((b, seq, n), dt)
    return pl.pallas_call(
        functools.partial(_inproj_latent_kernel, layer=layer),
        grid=(b, seq // tm),
        in_specs=[tok(d),
                  pl.BlockSpec((None, N_MOD, d), lambda i, t: (i, 0, 0)),
                  _const_spec((1, d)),
                  _const_spec(w_bf.shape),
                  _const_spec(lbl.shape),
                  _const_spec(qg.shape),
                  _const_spec(kg.shape),
                  pl.BlockSpec((tm, LANES), lambda i, t: (t, 0)),
                  pl.BlockSpec((tm, LANES), lambda i, t: (t, 0)),
                  _const_spec(gsum.shape)],
        out_specs=[tok(HG_KEY), tok(HG_KEY), tok(HG_KEY), tok(HG_KEY), tok(HG_KEY), tok(HG_VAL),
                   tok(HG_VAL), tok(ATT_Q), tok(ATT_KV), tok(ATT_KV), tok(d), tok(d)],
        out_shape=[shp(HG_KEY, BF16), shp(HG_KEY, BF16), shp(HG_KEY, F32), shp(HG_KEY, BF16),
                   shp(HG_KEY, F32), shp(HG_VAL, BF16), shp(HG_VAL, BF16), shp(ATT_Q, BF16),
                   shp(ATT_KV, BF16), shp(ATT_KV, BF16), shp(d, BF16), shp(d, BF16)],
        compiler_params=_params(("parallel", "parallel")),
        name="inproj_latent",
    )(x, mods, norm_g, w_bf, lbl, qg, kg, cos, sin, gsum)


def _inproj_context(ctx, mods, ctx_row, norm_g, w_bf, lbl, kg, gsum, layer):
    b, n_ctx, d = ctx.shape
    tm = CTX_TILE
    tok = lambda n: pl.BlockSpec((None, tm, n), lambda i, t: (i, t, 0))
    shp = lambda n, dt: jax.ShapeDtypeStruct((b, n_ctx, n), dt)
    return pl.pallas_call(
        functools.partial(_inproj_context_kernel, layer=layer),
        grid=(b, n_ctx // tm),
        in_specs=[tok(d),
                  pl.BlockSpec((None, N_MOD, d), lambda i, t: (ctx_row, 0, 0)),
                  _const_spec((1, d)),
                  _const_spec(w_bf.shape),
                  _const_spec(lbl.shape),
                  _const_spec(kg.shape),
                  _const_spec(gsum.shape)],
        out_specs=[tok(HG_KEY), tok(HG_KEY), tok(HG_KEY), tok(HG_KEY), tok(HG_VAL),
                   tok(ATT_KV), tok(ATT_KV)],
        out_shape=[shp(HG_KEY, BF16), shp(HG_KEY, F32), shp(HG_KEY, BF16), shp(HG_KEY, F32),
                   shp(HG_VAL, BF16), shp(ATT_KV, BF16), shp(ATT_KV, BF16)],
        compiler_params=_params(("parallel", "parallel")),
        name="inproj_context",
    )(ctx, mods, norm_g, w_bf, lbl, kg, gsum)


_HG_LEVELS = tuple(2 ** i for i in range(int(np.log2(HG_CHUNK))))


def _level_operand(h, ch):
    g, cum, cum_ref, reverse = ch["g"], ch["cum"], ch["cum_ref"], ch["reverse"]
    c = cum.shape[0]
    q32, k32 = ch["q32"], ch["k32"]
    if h >= SUBLANES:
        src, arg = [], []
        for p in range(c // (2 * h)):
            first = slice(p * 2 * h, p * 2 * h + h)
            second = slice(p * 2 * h + h, (p + 1) * 2 * h)
            r = p * 2 * h + (h if reverse else h - 1)
            mid = cum_ref[r:r + 1, :]
            if reverse:
                src += [q32[first], k32[second]]
                arg += [cum[first] - mid, mid - cum[second]]
            else:
                src += [k32[first], q32[second]]
                arg += [mid - cum[first], cum[second] - mid]
        return (jnp.concatenate(src, axis=0) * jnp.exp2(jnp.concatenate(arg, axis=0))).astype(BF16)
    row = lax.broadcasted_iota(I32, (c, 1), 0)
    u = row % (2 * h)
    query = (u < h) if reverse else (u >= h)
    src = jnp.where(query, q32, k32)
    if h == 1:
        arg = jnp.where(query, g, 0.0)
    elif h == 2:
        up = pltpu.roll(g, c - 1, 0)
        down = pltpu.roll(g, 1, 0)
        if reverse:
            arg = jnp.where(u == 0, g + up, jnp.where(u == 1, g, jnp.where(u == 2, 0.0, down)))
        else:
            arg = jnp.where(u == 0, up, jnp.where(u == 1, 0.0, jnp.where(u == 2, g, g + down)))
    else:
        pieces = []
        for p in range(c // (2 * h)):
            r = p * 2 * h + (h if reverse else h - 1)
            pieces.append(jnp.broadcast_to(cum_ref[r:r + 1, :], (2 * h, HG_DK)))
        arg = -jnp.abs(cum - jnp.concatenate(pieces, axis=0))
    return (src * jnp.exp2(arg)).astype(BF16)


def _hgrn_chunks(dirs, tri, level_id, want_o):
    c = HG_CHUNK
    nt = (((1,), (1,)), ((), ()))
    tn = (((0,), (0,)), ((), ()))
    chains = []
    for di, d in enumerate(dirs):
        g = d["g"]
        g_hi = g.astype(BF16)
        g_lo = (g - g_hi.astype(F32)).astype(BF16)
        t = tri[d["reverse"]]
        cum = (jnp.dot(t, g_hi, preferred_element_type=F32)
               + jnp.dot(t, g_lo, preferred_element_type=F32))
        n_heads = g.shape[1] // HG_DK
        for pi in range(n_heads // HG_STACK):
            heads = list(range(pi * HG_STACK, (pi + 1) * HG_STACK))

            def stack(a, heads=heads):
                return jnp.concatenate([a[:, hd * HG_DK:(hd + 1) * HG_DK] for hd in heads], axis=0)

            chains.append(dict(di=di, reverse=d["reverse"], heads=heads, cum=stack(cum),
                               g=stack(g), k=stack(d["k"]), v=stack(d["v"]),
                               q=None if d["q"] is None else stack(d["q"]),
                               st_refs=[d["st_ref"].at[hd] for hd in heads],
                               cum_ref=d["cum_ref"].at[pi]))
    for ch in chains:
        cum = ch["cum"]
        lasts = [cum[i * c:i * c + 1, :] if ch["reverse"] else cum[(i + 1) * c - 1:(i + 1) * c, :]
                 for i in range(HG_STACK)]
        last_rows = jnp.concatenate([jnp.broadcast_to(l, (c, HG_DK)) for l in lasts], axis=0)
        ch["k32"] = ch["k"].astype(F32)
        kl = (ch["k32"] * jnp.exp2(last_rows - cum)).astype(BF16)
        ch["st"] = []
        for i, st_ref in enumerate(ch["st_refs"]):
            rows = slice(i * c, (i + 1) * c)
            st = st_ref[...]
            st_ref[...] = st * jnp.exp2(lasts[i]) + lax.dot_general(
                ch["v"][rows], kl[rows], tn, preferred_element_type=F32)
            ch["st"].append(st)
    if not want_o:
        return None
    for ch in chains:
        ch["q32"] = ch["q"].astype(F32)
        ch["cum_ref"][...] = ch["cum"]
        ch["scores"] = jnp.zeros((HG_STACK * c, HG_STACK * c), F32)
    for li, h in enumerate(_HG_LEVELS):
        for ch in chains:
            m = _level_operand(h, ch)
            s_h = lax.dot_general(m, m, nt, preferred_element_type=F32)
            ch["scores"] = jnp.where(level_id[ch["reverse"]] == li, s_h, ch["scores"])
    outs = [[None] * (len(chains) * HG_STACK // len(dirs)) for _ in dirs]
    for ch in chains:
        self_term = jnp.sum(ch["q32"] * ch["k32"], axis=-1, keepdims=True)
        scores = jnp.where(level_id[ch["reverse"]] == len(_HG_LEVELS), self_term, ch["scores"])
        o = jnp.dot(scores.astype(BF16), ch["v"], preferred_element_type=F32)
        qd = (ch["q32"] * jnp.exp2(ch["cum"])).astype(BF16)
        for i, hd in enumerate(ch["heads"]):
            rows = slice(i * c, (i + 1) * c)
            outs[ch["di"]][hd] = o[rows] + lax.dot_general(
                qd[rows], ch["st"][i].astype(BF16), nt, preferred_element_type=F32)
    return outs


def _hgrn_kernel(q_ref, kf_ref, lf_ref, kb_ref, lb_ref, v_ref,
                 ckf_ref, clf_ref, ckb_ref, clb_ref, cv_ref, gn_ref, o_ref,
                 st_ref, of_ref, ob_ref, cum_ref):
    c = HG_CHUNK
    n_heads = q_ref.shape[1] // HG_DK
    n_ctx = ckf_ref.shape[0] // c
    n_lat = q_ref.shape[0] // c
    row = lax.broadcasted_iota(I32, (c, c), 0)
    col = lax.broadcasted_iota(I32, (c, c), 1)
    tri = {False: jnp.where(col <= row, 1.0, 0.0).astype(BF16),
           True: jnp.where(col >= row, 1.0, 0.0).astype(BF16)}
    rows2 = lax.broadcasted_iota(I32, (HG_STACK * c, HG_STACK * c), 0)
    cols2 = lax.broadcasted_iota(I32, (HG_STACK * c, HG_STACK * c), 1)
    x = rows2 ^ cols2
    lvl = jnp.where(x == 0, len(_HG_LEVELS), -1)
    for li, h in enumerate(_HG_LEVELS):
        lvl = jnp.where((x >= h) & (x < 2 * h), li, lvl)
    level_id = {False: jnp.where(cols2 <= rows2, lvl, -1), True: jnp.where(cols2 >= rows2, lvl, -1)}

    st_ref[...] = jnp.zeros_like(st_ref)

    def dirs_at(a, z, k_f, l_f, k_b, l_b, v, q):
        out = []
        for slot, (rev, start, kk, ll) in enumerate(((False, a, k_f, l_f), (True, z, k_b, l_b))):
            rows = pl.ds(start, c)
            out.append(dict(reverse=rev, g=ll[rows, :], k=kk[rows, :], v=v[rows, :],
                            q=None if q is None else q[rows, :],
                            st_ref=st_ref.at[slot], cum_ref=cum_ref.at[slot]))
        return out

    def ctx_body(j, carry):
        a = pl.multiple_of(j * c, c)
        z = pl.multiple_of((n_ctx - 1 - j) * c, c)
        _hgrn_chunks(dirs_at(a, z, ckf_ref, clf_ref, ckb_ref, clb_ref, cv_ref, None),
                     tri, level_id, False)
        return carry

    lax.fori_loop(0, n_ctx, ctx_body, 0)

    def lat_body(j, carry):
        a = pl.multiple_of(j * c, c)
        z = pl.multiple_of((n_lat - 1 - j) * c, c)
        outs = _hgrn_chunks(dirs_at(a, z, kf_ref, lf_ref, kb_ref, lb_ref, v_ref, q_ref),
                            tri, level_id, True)
        for hd in range(n_heads):
            lanes = slice(hd * HG_DV, (hd + 1) * HG_DV)
            of_ref[pl.ds(a, c), lanes] = outs[0][hd]
            ob_ref[pl.ds(z, c), lanes] = outs[1][hd]
        return carry

    lax.fori_loop(0, n_lat, lat_body, 0)

    for hd in range(n_heads):
        lanes = slice(hd * HG_DV, (hd + 1) * HG_DV)
        o = of_ref[:, lanes] + ob_ref[:, lanes]
        ms = jnp.mean(o * o, axis=-1, keepdims=True)
        o_ref[:, lanes] = (o * lax.rsqrt(ms + NORM_EPS) * gn_ref[:, lanes]).astype(BF16)


def _hgrn(hq, kf, lf, kb, lb, vi, ckf, clf, ckb, clb, cvi, gn):
    b, seq, _ = hq.shape
    n_ctx = ckf.shape[1]
    nh = HG_HEADS_PER_STEP
    lat = pl.BlockSpec((None, seq, nh * HG_DK), lambda i, h: (i, 0, h))
    cx = pl.BlockSpec((None, n_ctx, nh * HG_DK), lambda i, h: (i, 0, h))
    return pl.pallas_call(
        _hgrn_kernel,
        grid=(b, HG_HEADS // nh),
        in_specs=[lat, lat, lat, lat, lat, lat, cx, cx, cx, cx, cx,
                  pl.BlockSpec((1, nh * HG_DV), lambda i, h: (0, h))],
        out_specs=lat,
        out_shape=jax.ShapeDtypeStruct((b, seq, HG_VAL), BF16),
        scratch_shapes=[pltpu.VMEM((2, nh, HG_DV, HG_DK), F32),
                        pltpu.VMEM((seq, nh * HG_DV), F32), pltpu.VMEM((seq, nh * HG_DV), F32),
                        pltpu.VMEM((2, nh // HG_STACK, HG_STACK * HG_CHUNK, HG_DK), F32)],
        compiler_params=_params(("parallel", "parallel")),
        name="hgrn_scan",
    )(hq, kf, lf, kb, lb, vi, ckf, clf, ckb, clb, cvi, gn)


_HEADS_PER_TILE = LANES // ATT_HEAD_DIM


def _attn_kernel(q_ref, kx_ref, vx_ref, kc_ref, vc_ref, o_ref, kp_ref, vp_ref):
    kvh = pl.program_id(1)

    @pl.when(pl.program_id(2) == 0)
    def _build():
        k_all = jnp.concatenate([kc_ref[...], kx_ref[...]], axis=0)
        v_all = jnp.concatenate([vc_ref[...], vx_ref[...]], axis=0)
        r = lax.broadcasted_iota(I32, (ATT_KV, LANES), 0)
        cidx = lax.broadcasted_iota(I32, (ATT_KV, LANES), 1)
        for u in range(_HEADS_PER_TILE):
            place = (cidx // ATT_HEAD_DIM == u) & (r == kvh * ATT_HEAD_DIM + cidx - u * ATT_HEAD_DIM)
            rep = jnp.where(place, 1.0, 0.0).astype(BF16)
            kp_ref[u] = jnp.dot(k_all, rep, preferred_element_type=F32).astype(BF16)
            vp_ref[u] = jnp.dot(v_all, rep, preferred_element_type=F32).astype(BF16)

    units = [(slice(t * Q_TILE, (t + 1) * Q_TILE), slice(pair * LANES, (pair + 1) * LANES), u)
             for t in range(q_ref.shape[0] // Q_TILE)
             for pair in range(ATT_GROUPS // _HEADS_PER_TILE)
             for u in range(_HEADS_PER_TILE)]

    def scores(unit):
        rows, lanes, u = unit
        return lax.dot_general(q_ref[rows, lanes], kp_ref[u], (((1,), (1,)), ((), ())),
                               preferred_element_type=F32)

    s_next = scores(units[0])
    acc = None
    for i, (rows, lanes, u) in enumerate(units):
        s = s_next
        if i + 1 < len(units):
            s_next = scores(units[i + 1])
        p = jnp.exp2(s - jnp.max(s, axis=-1, keepdims=True))
        inv = 1.0 / jnp.sum(p, axis=-1, keepdims=True)
        part = jnp.dot(p.astype(BF16), vp_ref[u], preferred_element_type=F32) * inv
        acc = part if u == 0 else acc + part
        if u == _HEADS_PER_TILE - 1:
            o_ref[rows, lanes] = acc.astype(BF16)


def _attention(aq, ak, av, cak, cav):
    b, seq, _ = aq.shape
    n_ctx = cak.shape[1]
    width = ATT_GROUPS * ATT_HEAD_DIM
    kx = pl.BlockSpec((None, seq, ATT_KV), lambda i, h, t: (i, 0, 0))
    kc = pl.BlockSpec((None, n_ctx, ATT_KV), lambda i, h, t: (i, 0, 0))
    rows = Q_TILE * Q_TILES_PER_STEP
    assert seq % rows == 0
    qo = pl.BlockSpec((None, rows, width), lambda i, h, t: (i, t, h))
    return pl.pallas_call(
        _attn_kernel,
        grid=(b, ATT_KV_HEADS, seq // rows),
        in_specs=[qo, kx, kx, kc, kc],
        out_specs=qo,
        out_shape=jax.ShapeDtypeStruct((b, seq, ATT_Q), BF16),
        scratch_shapes=[pltpu.VMEM((_HEADS_PER_TILE, seq + n_ctx, LANES), BF16),
                        pltpu.VMEM((_HEADS_PER_TILE, seq + n_ctx, LANES), BF16)],
        compiler_params=_params(("parallel", "parallel", "arbitrary")),
        name="attention",
    )(aq, ak, av, cak, cav)


def _merge_kernel(on_ref, sg_ref, oa_ref, ga_ref, gb_ref, x_ref, mod_ref, n2_ref,
                  wa_ref, wb_ref, wo_ref, wr_ref, x1_ref, h2_ref, lg_ref):
    tm = x_ref.shape[0]
    parts = [slice(i * tm // MERGE_SUBTILES, (i + 1) * tm // MERGE_SUBTILES)
             for i in range(MERGE_SUBTILES)]
    a, bb, y = {}, {}, {}
    for i, rows in enumerate(parts):
        a_in = (on_ref[rows, :].astype(F32) * sg_ref[rows, :].astype(F32)).astype(BF16)
        a[i] = jnp.dot(a_in, wa_ref[...], preferred_element_type=F32)
        bb[i] = jnp.dot(oa_ref[rows, :], wb_ref[...], preferred_element_type=F32)
    for i, rows in enumerate(parts):
        m = ga_ref[rows, :].astype(F32) * a[i] + gb_ref[rows, :].astype(F32) * bb[i]
        y[i] = jnp.dot(m.astype(BF16), wo_ref[...], preferred_element_type=F32)
    wr = wr_ref[...]
    for i, rows in enumerate(parts):
        x1 = x_ref[rows, :] + mod_ref[2:3, :] * y[i]
        x1_ref[rows, :] = x1
        h2 = _rms_mod(x1, n2_ref[...], mod_ref[3:4, :], mod_ref[4:5, :])
        h_hi = h2.astype(BF16)
        h_lo = (h2 - h_hi.astype(F32)).astype(BF16)
        h2_ref[rows, :] = h_hi
        r = (jnp.dot(h_hi, wr, preferred_element_type=F32)
             + jnp.dot(h_lo, wr, preferred_element_type=F32))
        lg_ref[rows, :] = r[:, :N_EXPERTS] + r[:, N_EXPERTS:2 * N_EXPERTS]


def _merge(on, sg, oatt, sga, sgb, x, mods, norm2_g, wa, wb, wo, wr):
    b, seq, d = x.shape
    tm = TOKEN_TILE
    tok = lambda n: pl.BlockSpec((None, tm, n), lambda i, t: (i, t, 0))
    return pl.pallas_call(
        _merge_kernel,
        grid=(b, seq // tm),
        in_specs=[tok(HG_VAL), tok(HG_VAL), tok(ATT_Q), tok(d), tok(d), tok(d),
                  pl.BlockSpec((None, N_MOD, d), lambda i, t: (i, 0, 0)),
                  _const_spec((1, d)), _const_spec(wa.shape), _const_spec(wb.shape),
                  _const_spec(wo.shape), _const_spec(wr.shape)],
        out_specs=[tok(d), tok(d), tok(N_EXPERTS)],
        out_shape=[jax.ShapeDtypeStruct((b, seq, d), F32),
                   jax.ShapeDtypeStruct((b, seq, d), BF16),
                   jax.ShapeDtypeStruct((b, seq, N_EXPERTS), F32)],
        compiler_params=_params(("parallel", "parallel")),
        name="merge",
    )(on, sg, oatt, sga, sgb, x, mods, norm2_g, wa, wb, wo, wr)


def _route_kernel(lg_ref, pos_ref, gate_ref, starts_ref, *, cap):
    lg = lg_ref[...]
    n_b, n_exp, length = lg.shape
    e = jnp.exp(lg - jnp.max(lg, axis=1, keepdims=True))
    aff = (e / jnp.sum(e, axis=1, keepdims=True)).reshape(n_b * n_exp, length)
    n_e = n_b * n_exp
    def count(mask):
        return jnp.sum(jnp.where(mask, 1.0, 0.0), axis=1, keepdims=True)

    def value_step(i, t):
        cand = t | lax.shift_left(jnp.int32(1), jnp.asarray(30 - i, I32))
        enough = count(aff >= lax.bitcast_convert_type(cand, F32)) >= cap
        return jnp.where(enough, cand, t)

    thr = lax.bitcast_convert_type(
        lax.fori_loop(0, 31, value_step, jnp.zeros((n_e, 1), I32)), F32)
    above = aff > thr
    tied = aff == thr
    need = cap - count(above)
    idx = lax.broadcasted_iota(I32, (1, length), 1)
    n_bits = int(np.log2(length))

    def index_step(i, j):
        cand = j | lax.shift_left(jnp.int32(1), jnp.asarray(n_bits - 1 - i, I32))
        return jnp.where(count(tied & (idx < cand)) < need, cand, j)

    last = lax.fori_loop(0, n_bits, index_step, jnp.zeros((n_e, 1), I32))
    sel = above | (tied & (idx <= last))

    r = lax.broadcasted_iota(I32, (LANES, LANES), 0)
    cc = lax.broadcasted_iota(I32, (LANES, LANES), 1)
    before = jnp.where(r < cc, 1.0, 0.0).astype(BF16)
    sel_b = jnp.where(sel, 1.0, 0.0).astype(BF16)
    offset = jnp.zeros((n_e, 1), F32)
    pieces = []
    groups_per_tile = ROUTE_TILE // LANES
    lane = lax.broadcasted_iota(I32, (1, LANES), 1)
    starts = jnp.zeros((n_e, LANES), F32)
    for gidx in range(length // LANES):
        blk = sel_b[:, gidx * LANES:(gidx + 1) * LANES]
        pieces.append(jnp.dot(blk, before, preferred_element_type=F32) + offset)
        offset = offset + jnp.sum(blk.astype(F32), axis=1, keepdims=True)
        if (gidx + 1) % groups_per_tile == 0:
            starts = jnp.where(lane == (gidx + 1) // groups_per_tile, offset, starts)
    pos = jnp.concatenate(pieces, axis=1)
    pos_ref[...] = jnp.where(sel, pos.astype(I32), -1).reshape(n_b, n_exp, length)
    gate_ref[...] = jnp.where(sel, aff, 0.0).reshape(n_b, n_exp, length)
    starts_ref[...] = starts.astype(I32).reshape(n_b, n_exp, LANES)


def _route(logits_t, cap):
    b, n_e, length = logits_t.shape
    assert length // ROUTE_TILE < LANES and n_e % SUBLANES == 0
    spec = _const_spec((b, n_e, length))
    return pl.pallas_call(
        functools.partial(_route_kernel, cap=cap),
        grid=(1,),
        in_specs=[spec],
        out_specs=[spec, spec, _const_spec((b, n_e, LANES))],
        out_shape=[jax.ShapeDtypeStruct((b, n_e, length), I32),
                   jax.ShapeDtypeStruct((b, n_e, length), F32),
                   jax.ShapeDtypeStruct((b, n_e, LANES), I32)],
        compiler_params=_params(("arbitrary",)),
        name="route",
    )(logits_t)


def _tile_windows(starts_ref, sample, tile, n_e, n_tiles):
    stride = n_tiles + 1
    lows, n_win = [], jnp.int32(0)
    for e in range(n_e):
        at = (sample * n_e + e) * stride + tile
        low = (starts_ref[at] // BF16_ROWS) * BF16_ROWS
        lows.append(low)
        n_win = jnp.maximum(n_win, (starts_ref[at + 1] - low + ROUTE_WINDOW - 1) // ROUTE_WINDOW)
    return lows, n_win


def _window(low, w, cap):
    lo = low + w * ROUTE_WINDOW
    start = pl.multiple_of(jnp.minimum(lo, cap - ROUTE_WINDOW), BF16_ROWS)
    slot = start + lax.broadcasted_iota(I32, (ROUTE_WINDOW, 1), 0)
    return start, slot, slot >= lo


def _gather_kernel(starts_ref, pos_ref, h_ref, xe_ref):
    n_e, cap, d = xe_ref.shape
    n_tiles = h_ref.shape[0] // ROUTE_TILE
    sample = pl.program_id(0)
    xe_ref[...] = jnp.zeros_like(xe_ref)

    def window(lows, tokens, w):
        pieces, begins = [], []
        for e in range(n_e):
            start, slot, live = _window(lows[e], w, cap)
            hit = (pos_ref[e:e + 1, tokens] == slot) & live
            pieces.append(jnp.where(hit, 1.0, 0.0).astype(BF16))
            begins.append(start)
        onehot = jnp.concatenate(pieces, axis=0)
        rows = jnp.dot(onehot, h_ref[tokens, :], preferred_element_type=F32).astype(BF16)
        return begins, rows

    def accumulate(begins, rows):
        for e in range(n_e):
            dst = xe_ref.at[e, pl.ds(begins[e], ROUTE_WINDOW), :]
            dst[...] = dst[...] + rows[e * ROUTE_WINDOW:(e + 1) * ROUTE_WINDOW]

    tiles = []
    for j in range(n_tiles):
        tokens = slice(j * ROUTE_TILE, (j + 1) * ROUTE_TILE)
        lows, n_win = _tile_windows(starts_ref, sample, j, n_e, n_tiles)
        tiles.append((tokens, lows, n_win))
        accumulate(*window(lows, tokens, 0))
    for tokens, lows, n_win in tiles:
        def more(w, carry, tokens=tokens, lows=lows):
            accumulate(*window(lows, tokens, w))
            return carry

        lax.fori_loop(1, n_win, more, 0)


def _gather(starts, pos, h2, cap):
    b, n_e, length = pos.shape
    d = h2.shape[-1]
    assert length % ROUTE_TILE == 0 and cap % BF16_ROWS == 0 and cap >= ROUTE_WINDOW
    grid_spec = pltpu.PrefetchScalarGridSpec(
        num_scalar_prefetch=1,
        grid=(b,),
        in_specs=[pl.BlockSpec((None, n_e, length), lambda i, st: (i, 0, 0)),
                  pl.BlockSpec((None, length, d), lambda i, st: (i, 0, 0))],
        out_specs=pl.BlockSpec((None, n_e, cap, d), lambda i, st: (i, 0, 0, 0)))
    return pl.pallas_call(
        _gather_kernel,
        grid_spec=grid_spec,
        out_shape=jax.ShapeDtypeStruct((b, n_e, cap, d), BF16),
        compiler_params=_params(("parallel",)),
        name="gather",
    )(starts, pos, h2)


def _ffn_kernel(xe_ref, wg_ref, wu_ref, wd_ref, ye_ref):
    nb, cap, d = xe_ref.shape
    x = xe_ref[...].reshape(nb * cap, d)
    hg = jnp.dot(x, wg_ref[...].astype(BF16), preferred_element_type=F32)
    hu = jnp.dot(x, wu_ref[...].astype(BF16), preferred_element_type=F32)
    hid = (hg * jax.nn.sigmoid(hg) * hu).astype(BF16)
    ye = jnp.dot(hid, wd_ref[...].astype(BF16), preferred_element_type=F32)
    ye_ref[...] = ye.astype(BF16).reshape(nb, cap, d)


def _ffn(xe, wg, wu, wd):
    b, n_e, cap, d = xe.shape
    ff = wg.shape[-1]
    nb = int(np.gcd(b, FFN_BATCH_GROUP))
    tok = pl.BlockSpec((nb, None, cap, d), lambda e, i: (i, e, 0, 0))
    return pl.pallas_call(
        _ffn_kernel,
        grid=(n_e, b // nb),
        in_specs=[tok,
                  pl.BlockSpec((None, d, ff), lambda e, i: (e, 0, 0)),
                  pl.BlockSpec((None, d, ff), lambda e, i: (e, 0, 0)),
                  pl.BlockSpec((None, ff, d), lambda e, i: (e, 0, 0))],
        out_specs=tok,
        out_shape=jax.ShapeDtypeStruct((b, n_e, cap, d), BF16),
        compiler_params=_params(("parallel", "arbitrary")),
        name="expert_ffn",
    )(xe, wg, wu, wd)


def _combine_kernel(starts_ref, pos_ref, gate_ref, ye_ref, x1_ref, mod_ref, fg_ref, o_ref):
    n_e, cap, _ = ye_ref.shape
    n_sub = x1_ref.shape[0] // ROUTE_TILE
    n_tiles = pl.num_programs(1) * n_sub

    def window(lows, tokens, w):
        pieces, rows = [], []
        for e in range(n_e):
            start, slot, live = _window(lows[e], w, cap)
            hit = (pos_ref[e:e + 1, tokens] == slot) & live
            pieces.append(jnp.where(hit, gate_ref[e:e + 1, tokens], 0.0).astype(BF16))
            rows.append(ye_ref[e, pl.ds(start, ROUTE_WINDOW), :])
        scatter = jnp.concatenate(pieces, axis=0)
        return lax.dot_general(scatter, jnp.concatenate(rows, axis=0),
                               (((0,), (0,)), ((), ())), preferred_element_type=F32)

    def finish(tokens, y):
        x2 = x1_ref[tokens, :] + mod_ref[5:6, :] * y
        ms = jnp.mean(x2 * x2, axis=-1, keepdims=True)
        o_ref[tokens, :] = x2 * lax.rsqrt(ms + NORM_EPS) * fg_ref[...]

    subs = []
    for s in range(n_sub):
        tokens = slice(s * ROUTE_TILE, (s + 1) * ROUTE_TILE)
        lows, n_win = _tile_windows(starts_ref, pl.program_id(0), pl.program_id(1) * n_sub + s,
                                    n_e, n_tiles)
        subs.append((tokens, lows, n_win, window(lows, tokens, 0)))
    for tokens, lows, n_win, y in subs:
        finish(tokens, y)
    for tokens, lows, n_win, y in subs:
        @pl.when(n_win > 1)
        def _more(tokens=tokens, lows=lows, n_win=n_win, y=y):
            finish(tokens, lax.fori_loop(1, n_win, lambda w, a: a + window(lows, tokens, w), y))


def _combine(starts, pos, gate, ye, x1, mods, final_g):
    b, seq, d = x1.shape
    n_e, cap = ye.shape[1], ye.shape[2]
    tm = ROUTE_TILE * COMBINE_SUBTILES
    tok = lambda n: pl.BlockSpec((None, tm, n), lambda i, t, st: (i, t, 0))
    sel = pl.BlockSpec((None, n_e, tm), lambda i, t, st: (i, 0, t))
    grid_spec = pltpu.PrefetchScalarGridSpec(
        num_scalar_prefetch=1,
        grid=(b, seq // tm),
        in_specs=[sel, sel,
                  pl.BlockSpec((None, n_e, cap, d), lambda i, t, st: (i, 0, 0, 0)),
                  tok(d),
                  pl.BlockSpec((None, N_MOD, d), lambda i, t, st: (i, 0, 0)),
                  pl.BlockSpec((1, d), lambda i, t, st: (0, 0))],
        out_specs=tok(d))
    return pl.pallas_call(
        _combine_kernel,
        grid_spec=grid_spec,
        out_shape=jax.ShapeDtypeStruct((b, seq, d), F32),
        compiler_params=_params(("parallel", "arbitrary")),
        name="combine",
    )(starts, pos, gate, ye, x1, mods, final_g)


def _rope_tables(length):
    rows = length // GRID_W
    row = jnp.repeat(jnp.arange(rows, dtype=F32), GRID_W)
    col = jnp.tile(jnp.arange(GRID_W, dtype=F32), rows)
    inv_freq = ROPE_THETA ** (-jnp.arange(0, ROPE_AXIS_DIM, 2, dtype=F32) / ROPE_AXIS_DIM)
    half = ROPE_AXIS_DIM // 2
    cos_parts, sin_parts = [], []
    for pos in (row, col):
        ang = pos[:, None] * inv_freq
        cos_parts += [jnp.cos(ang), jnp.cos(ang)]
        sin_parts += [-jnp.sin(ang), jnp.sin(ang)]
    cos = jnp.concatenate(cos_parts, axis=-1)
    sin = jnp.concatenate(sin_parts, axis=-1)
    reps = LANES // ATT_HEAD_DIM
    assert half * 4 == ATT_HEAD_DIM
    return jnp.tile(cos, (1, reps)), jnp.tile(sin, (1, reps))


def kernel(x, c, ctx, c_ctx, w_mod, b_mod, norm1_g, norm2_g, w_in, hg_lb_logits, hg_norm_g,
           q_norm_g, k_norm_g, w_branch_a, w_branch_b, w_out, w_router, w_exp_gate, w_exp_up,
           w_exp_down, final_norm_g):
    b, seq, d = x.shape
    depth = w_mod.shape[0]
    assert depth == 1, "context-stream update between layers is not implemented"
    layer = 0
    cap = EC_CAPACITY_FACTOR * seq // N_EXPERTS

    n_rows = -(-(b + 1) // SUBLANES) * SUBLANES
    cond = jnp.zeros((n_rows, d), F32).at[:b].set(c).at[b].set(c_ctx)
    mods = _adaln(cond, w_mod[layer], b_mod[layer]).reshape(n_rows, N_MOD, d)

    cos, sin = _rope_tables(seq)
    gsum = jnp.asarray(np.kron(np.eye(ATT_HEADS), np.ones((ATT_HEAD_DIM, ATT_HEAD_DIM))), BF16)
    qg = jnp.tile(q_norm_g[layer], ATT_HEADS).reshape(1, ATT_Q)
    kg = jnp.tile(k_norm_g[layer], ATT_KV_HEADS).reshape(1, ATT_KV)
    w_bf = w_in[layer].astype(BF16)
    w_ctx = jnp.concatenate([w_bf[:, _OFF_FF:_OFF_G], w_bf[:, _OFF_AK:_OFF_AV + ATT_KV]], axis=1)
    n1 = norm1_g[layer].reshape(1, d)

    hq, kf, lf, kb, lb, vi, sg, aq, ak, av, sga, sgb = _inproj_latent(
        x, mods, n1, w_bf, hg_lb_logits, qg, kg, cos, sin, gsum, layer)
    ckf, clf, ckb, clb, cvi, cak, cav = _inproj_context(
        ctx, mods, b, n1, w_ctx, hg_lb_logits, kg, gsum[:ATT_KV, :ATT_KV], layer)

    on = _hgrn(hq, kf, lf, kb, lb, vi, ckf, clf, ckb, clb, cvi,
               hg_norm_g[layer].reshape(1, HG_VAL))
    oatt = _attention(aq, ak, av, cak, cav)

    wr_hi = w_router[layer].astype(BF16)
    wr_lo = (w_router[layer] - wr_hi.astype(F32)).astype(BF16)
    wr = jnp.zeros((d, LANES), BF16).at[:, :N_EXPERTS].set(wr_hi)
    wr = wr.at[:, N_EXPERTS:2 * N_EXPERTS].set(wr_lo)
    x1, h2, logits = _merge(on, sg, oatt, sga, sgb, x, mods, norm2_g[layer].reshape(1, d),
                            w_branch_a[layer].astype(BF16), w_branch_b[layer].astype(BF16),
                            w_out[layer].astype(BF16), wr)

    pos, gate, starts = _route(jnp.swapaxes(logits, 1, 2), cap)
    starts = starts[:, :, :seq // ROUTE_TILE + 1].reshape(-1)
    xe = _gather(starts, pos, h2, cap)
    ye = _ffn(xe, w_exp_gate[layer], w_exp_up[layer], w_exp_down[layer])
    return _combine(starts, pos, gate, ye, x1, mods, final_norm_g.reshape(1, d))
```

```python
import functools

import numpy as np
import jax
import jax.numpy as jnp
from jax import lax
from jax.experimental import pallas as pl
from jax.experimental.pallas import tpu as pltpu

F32 = jnp.float32
BF16 = jnp.bfloat16
I32 = jnp.int32

GRID_W = 64
HG_HEADS = 4
HG_DK = 128
HG_DV = 128
HG_KEY = HG_HEADS * HG_DK
HG_VAL = HG_HEADS * HG_DV
ATT_HEADS = 8
ATT_KV_HEADS = 2
ATT_HEAD_DIM = 64
ATT_GROUPS = ATT_HEADS // ATT_KV_HEADS
ATT_Q = ATT_HEADS * ATT_HEAD_DIM
ATT_KV = ATT_KV_HEADS * ATT_HEAD_DIM
ROPE_AXIS_DIM = ATT_HEAD_DIM // 2
ROPE_THETA = 10000.0
N_EXPERTS = 16
EC_CAPACITY_FACTOR = 2
N_MOD = 6
NORM_EPS = 1e-6
LOG2_E = float(np.log2(np.e))

LANES = 128
SUBLANES = 8
BF16_ROWS = 16
VMEM_LIMIT_BYTES = 56 * 1024 * 1024

HG_CHUNK = 128
HG_HEADS_PER_STEP = 4
HG_STACK = 1
TOKEN_TILE = 512
CTX_TILE = 256
MERGE_SUBTILES = 2
Q_TILE = 512
Q_TILES_PER_STEP = 2
FFN_BATCH_GROUP = 4
ROUTE_TILE = 256
ROUTE_WINDOW = 64
COMBINE_SUBTILES = 4

_OFF_HQ = 0
_OFF_FF = _OFF_HQ + HG_KEY
_OFF_FB = _OFF_FF + HG_KEY
_OFF_I = _OFF_FB + HG_KEY
_OFF_G = _OFF_I + HG_VAL
_OFF_AQ = _OFF_G + HG_VAL
_OFF_AK = _OFF_AQ + ATT_Q
_OFF_AV = _OFF_AK + ATT_KV


def _params(sem):
    return pltpu.CompilerParams(dimension_semantics=sem, vmem_limit_bytes=VMEM_LIMIT_BYTES)


def _const_spec(shape):
    zeros = (0,) * len(shape)
    return pl.BlockSpec(shape, lambda *_: zeros)


def _adaln_kernel(c_ref, w_ref, b_ref, o_ref):
    c = c_ref[...]
    a = c * jax.nn.sigmoid(c)
    o_ref[...] = jnp.dot(a, w_ref[...], preferred_element_type=F32,
                         precision=lax.Precision.HIGHEST) + b_ref[...]


def _adaln(cond, w_mod, b_mod):
    n, d = cond.shape
    ncol = w_mod.shape[1]
    return pl.pallas_call(
        _adaln_kernel,
        grid=(ncol // d,),
        in_specs=[pl.BlockSpec((n, d), lambda j: (0, 0)),
                  pl.BlockSpec((d, d), lambda j: (0, j)),
                  pl.BlockSpec((1, d), lambda j: (0, j))],
        out_specs=pl.BlockSpec((n, d), lambda j: (0, j)),
        out_shape=jax.ShapeDtypeStruct((n, ncol), F32),
        compiler_params=_params(("arbitrary",)),
        name="adaln",
    )(cond, w_mod, b_mod.reshape(1, ncol))


def _rms_mod(x, gain, shift, scale):
    ms = jnp.mean(x * x, axis=-1, keepdims=True)
    h = x * lax.rsqrt(ms + NORM_EPS) * gain
    return h * (1.0 + scale) + shift


def _lower_bound(lbl_ref, direction, layer):
    lg = lbl_ref[direction]
    e = jnp.exp(lg - jnp.max(lg, axis=0, keepdims=True))
    p = e / jnp.sum(e, axis=0, keepdims=True)
    return jnp.sum(p[:layer + 1], axis=0, keepdims=True)


def _hgrn_gate(f_pre, lb):
    s = jax.nn.sigmoid(f_pre)
    return (1.0 - lb) * (1.0 - s), jnp.log2(lb + (1.0 - lb) * s)


def _head_rms(a, gsum, gain):
    ssq = jnp.dot((a * a).astype(BF16), gsum, preferred_element_type=F32)
    return a * lax.rsqrt(ssq * (1.0 / ATT_HEAD_DIM) + NORM_EPS) * gain


def _rope(a, cos, sin):
    lane = lax.broadcasted_iota(I32, (1, LANES), 1)
    first = (lane % ROPE_AXIS_DIM) < (ROPE_AXIS_DIM // 2)
    half = ROPE_AXIS_DIM // 2
    outs = []
    for j in range(a.shape[1] // LANES):
        xg = a[:, j * LANES:(j + 1) * LANES]
        partner = jnp.where(first, pltpu.roll(xg, LANES - half, 1), pltpu.roll(xg, half, 1))
        outs.append(xg * cos + partner * sin)
    return outs[0] if len(outs) == 1 else jnp.concatenate(outs, axis=1)


def _inproj_latent_kernel(x_ref, mod_ref, g_ref, w_ref, lbl_ref, qg_ref, kg_ref, cos_ref, sin_ref,
                          gs_ref, hq_ref, kf_ref, lf_ref, kb_ref, lb_ref, vi_ref, sg_ref,
                          aq_ref, ak_ref, av_ref, ga_ref, gb_ref, *, layer):
    hb = _rms_mod(x_ref[...], g_ref[...], mod_ref[0:1, :], mod_ref[1:2, :]).astype(BF16)

    def proj(lo, n):
        return jnp.dot(hb, w_ref[:, lo:lo + n], preferred_element_type=F32)

    aq = proj(_OFF_AQ, ATT_Q)
    ak = proj(_OFF_AK, ATT_KV)
    key, logf = _hgrn_gate(proj(_OFF_FF, HG_KEY), _lower_bound(lbl_ref, 0, layer))
    kf_ref[...] = key.astype(BF16)
    lf_ref[...] = logf
    key, logf = _hgrn_gate(proj(_OFF_FB, HG_KEY), _lower_bound(lbl_ref, 1, layer))
    kb_ref[...] = key.astype(BF16)
    lb_ref[...] = logf
    d = ga_ref.shape[-1]
    ga_ref[...] = jax.nn.sigmoid(proj(_OFF_AV + ATT_KV, d)).astype(BF16)
    gb_ref[...] = jax.nn.sigmoid(proj(_OFF_AV + ATT_KV + d, d)).astype(BF16)
    g = proj(_OFF_G, HG_VAL)
    sg_ref[...] = (g * jax.nn.sigmoid(g)).astype(BF16)
    cos = cos_ref[...]
    sin = sin_ref[...]
    aq = _head_rms(aq, gs_ref[...], qg_ref[...])
    aq_ref[...] = (_rope(aq, cos, sin) * (ATT_HEAD_DIM ** -0.5 * LOG2_E)).astype(BF16)
    ak = _head_rms(ak, gs_ref[0:ATT_KV, 0:ATT_KV], kg_ref[...])
    ak_ref[...] = _rope(ak, cos, sin).astype(BF16)
    av_ref[...] = proj(_OFF_AV, ATT_KV).astype(BF16)
    vi_ref[...] = proj(_OFF_I, HG_VAL).astype(BF16)
    hq_ref[...] = proj(_OFF_HQ, HG_KEY).astype(BF16)


def _inproj_context_kernel(x_ref, mod_ref, g_ref, w_ref, lbl_ref, kg_ref, gs_ref,
                           kf_ref, lf_ref, kb_ref, lb_ref, vi_ref, ak_ref, av_ref, *, layer):
    hb = _rms_mod(x_ref[...], g_ref[...], mod_ref[0:1, :], mod_ref[1:2, :]).astype(BF16)

    def proj(lo, n):
        return jnp.dot(hb, w_ref[:, lo:lo + n], preferred_element_type=F32)

    key, logf = _hgrn_gate(proj(0, HG_KEY), _lower_bound(lbl_ref, 0, layer))
    kf_ref[...] = key.astype(BF16)
    lf_ref[...] = logf
    key, logf = _hgrn_gate(proj(HG_KEY, HG_KEY), _lower_bound(lbl_ref, 1, layer))
    kb_ref[...] = key.astype(BF16)
    lb_ref[...] = logf
    vi_ref[...] = proj(2 * HG_KEY, HG_VAL).astype(BF16)
    ak = _head_rms(proj(2 * HG_KEY + HG_VAL, ATT_KV), gs_ref[...], kg_ref[...])
    ak_ref[...] = ak.astype(BF16)
    av_ref[...] = proj(2 * HG_KEY + HG_VAL + ATT_KV, ATT_KV).astype(BF16)


def _inproj_latent(x, mods, norm_g, w_bf, lbl, qg, kg, cos, sin, gsum, layer):
    b, seq, d = x.shape
    tm = TOKEN_TILE
    tok = lambda n: pl.BlockSpec((None, tm, n), lambda i, t: (i, t, 0))
    shp = lambda n, dt: jax.ShapeDtypeStruct((b, seq, n), dt)
    return pl.pallas_call(
        functools.partial(_inproj_latent_kernel, layer=layer),
        grid=(b, seq // tm),
        in_specs=[tok(d),
                  pl.BlockSpec((None, N_MOD, d), lambda i, t: (i, 0, 0)),
                  _const_spec((1, d)),
                  _const_spec(w_bf.shape),
                  _const_spec(lbl.shape),
                  _const_spec(qg.shape),
                  _const_spec(kg.shape),
                  pl.BlockSpec((tm, LANES), lambda i, t: (t, 0)),
                  pl.BlockSpec((tm, LANES), lambda i, t: (t, 0)),
                  _const_spec(gsum.shape)],
        out_specs=[tok(HG_KEY), tok(HG_KEY), tok(HG_KEY), tok(HG_KEY), tok(HG_KEY), tok(HG_VAL),
                   tok(HG_VAL), tok(ATT_Q), tok(ATT_KV), tok(ATT_KV), tok(d), tok(d)],
        out_shape=[shp(HG_KEY, BF16), shp(HG_KEY, BF16), shp(HG_KEY, F32), shp(HG_KEY, BF16),
                   shp(HG_KEY, F32), shp(HG_VAL, BF16), shp(HG_VAL, BF16), shp(ATT_Q, BF16),
                   shp(ATT_KV, BF16), shp(ATT_KV, BF16), shp(d, BF16), shp(d, BF16)],
        compiler_params=_params(("parallel", "parallel")),
        name="inproj_latent",
    )(x, mods, norm_g, w_bf, lbl, qg, kg, cos, sin, gsum)


def _inproj_context(ctx, mods, ctx_row, norm_g, w_bf, lbl, kg, gsum, layer):
    b, n_ctx, d = ctx.shape
    tm = CTX_TILE
    tok = lambda n: pl.BlockSpec((None, tm, n), lambda i, t: (i, t, 0))
    shp = lambda n, dt: jax.ShapeDtypeStruct((b, n_ctx, n), dt)
    return pl.pallas_call(
        functools.partial(_inproj_context_kernel, layer=layer),
        grid=(b, n_ctx // tm),
        in_specs=[tok(d),
                  pl.BlockSpec((None, N_MOD, d), lambda i, t: (ctx_row, 0, 0)),
                  _const_spec((1, d)),
                  _const_spec(w_bf.shape),
                  _const_spec(lbl.shape),
                  _const_spec(kg.shape),
                  _const_spec(gsum.shape)],
        out_specs=[tok(HG_KEY), tok(HG_KEY), tok(HG_KEY), tok(HG_KEY), tok(HG_VAL),
                   tok(ATT_KV), tok(ATT_KV)],
        out_shape=[shp(HG_KEY, BF16), shp(HG_KEY, F32), shp(HG_KEY, BF16), shp(HG_KEY, F32),
                   shp(HG_VAL, BF16), shp(ATT_KV, BF16), shp(ATT_KV, BF16)],
        compiler_params=_params(("parallel", "parallel")),
        name="inproj_context",
    )(ctx, mods, norm_g, w_bf, lbl, kg, gsum)


_HG_LEVELS = tuple(2 ** i for i in range(int(np.log2(HG_CHUNK))))


def _level_operand(h, ch):
    g, cum, reverse = ch["g"], ch["cum"], ch["reverse"]
    c = cum.shape[0]
    q32, k32 = ch["q32"], ch["k32"]
    if h >= SUBLANES:
        src, arg = [], []
        for p in range(c // (2 * h)):
            first = slice(p * 2 * h, p * 2 * h + h)
            second = slice(p * 2 * h + h, (p + 1) * 2 * h)
            r = p * 2 * h + (h if reverse else h - 1)
            mid = cum[r:r + 1, :]
            if reverse:
                src += [q32[first], k32[second]]
                arg += [cum[first] - mid, mid - cum[second]]
            else:
                src += [k32[first], q32[second]]
                arg += [mid - cum[first], cum[second] - mid]
        return (jnp.concatenate(src, axis=0) * jnp.exp2(jnp.concatenate(arg, axis=0))).astype(BF16)
    row = lax.broadcasted_iota(I32, (c, 1), 0)
    u = row % (2 * h)
    query = (u < h) if reverse else (u >= h)
    src = jnp.where(query, q32, k32)
    if h == 1:
        arg = jnp.where(query, g, 0.0)
    elif h == 2:
        up = pltpu.roll(g, c - 1, 0)
        down = pltpu.roll(g, 1, 0)
        if reverse:
            arg = jnp.where(u == 0, g + up, jnp.where(u == 1, g, jnp.where(u == 2, 0.0, down)))
        else:
            arg = jnp.where(u == 0, up, jnp.where(u == 1, 0.0, jnp.where(u == 2, g, g + down)))
    else:
        pieces = []
        for p in range(c // (2 * h)):
            r = p * 2 * h + (h if reverse else h - 1)
            pieces.append(jnp.broadcast_to(cum[r:r + 1, :], (2 * h, HG_DK)))
        arg = -jnp.abs(cum - jnp.concatenate(pieces, axis=0))
    return (src * jnp.exp2(arg)).astype(BF16)


def _hgrn_chunks(dirs, tri, level_id, want_o):
    c = HG_CHUNK
    nt = (((1,), (1,)), ((), ()))
    tn = (((0,), (0,)), ((), ()))
    chains = []
    for di, d in enumerate(dirs):
        g = d["g"]
        g_hi = g.astype(BF16)
        g_lo = (g - g_hi.astype(F32)).astype(BF16)
        cum = jnp.dot(tri[d["reverse"]], jnp.concatenate([g_hi, g_lo], axis=0),
                      preferred_element_type=F32)
        n_heads = g.shape[1] // HG_DK
        for pi in range(n_heads // HG_STACK):
            heads = list(range(pi * HG_STACK, (pi + 1) * HG_STACK))

            def stack(a, heads=heads):
                return jnp.concatenate([a[:, hd * HG_DK:(hd + 1) * HG_DK] for hd in heads], axis=0)

            chains.append(dict(di=di, reverse=d["reverse"], heads=heads, cum=stack(cum),
                               g=stack(g), k=stack(d["k"]), v=stack(d["v"]),
                               q=None if d["q"] is None else stack(d["q"]),
                               st_refs=[d["st_ref"].at[hd] for hd in heads]))
    for ch in chains:
        cum = ch["cum"]
        lasts = [cum[i * c:i * c + 1, :] if ch["reverse"] else cum[(i + 1) * c - 1:(i + 1) * c, :]
                 for i in range(HG_STACK)]
        last_rows = jnp.concatenate([jnp.broadcast_to(l, (c, HG_DK)) for l in lasts], axis=0)
        ch["k32"] = ch["k"].astype(F32)
        kl = (ch["k32"] * jnp.exp2(last_rows - cum)).astype(BF16)
        ch["st"] = []
        for i, st_ref in enumerate(ch["st_refs"]):
            rows = slice(i * c, (i + 1) * c)
            st = st_ref[...]
            st_ref[...] = st * jnp.exp2(lasts[i]) + lax.dot_general(
                ch["v"][rows], kl[rows], tn, preferred_element_type=F32)
            ch["st"].append(st)
    if not want_o:
        return None
    for ch in chains:
        ch["q32"] = ch["q"].astype(F32)
        ch["scores"] = jnp.zeros((HG_STACK * c, HG_STACK * c), F32)
    for li, h in enumerate(_HG_LEVELS):
        for ch in chains:
            m = _level_operand(h, ch)
            s_h = lax.dot_general(m, m, nt, preferred_element_type=F32)
            ch["scores"] = jnp.where(level_id[ch["reverse"]] == li, s_h, ch["scores"])
    outs = [[None] * (len(chains) * HG_STACK // len(dirs)) for _ in dirs]
    for ch in chains:
        self_term = jnp.sum(ch["q32"] * ch["k32"], axis=-1, keepdims=True)
        scores = jnp.where(level_id[ch["reverse"]] == len(_HG_LEVELS), self_term, ch["scores"])
        o = jnp.dot(scores.astype(BF16), ch["v"], preferred_element_type=F32)
        qd = (ch["q32"] * jnp.exp2(ch["cum"])).astype(BF16)
        for i, hd in enumerate(ch["heads"]):
            rows = slice(i * c, (i + 1) * c)
            outs[ch["di"]][hd] = o[rows] + lax.dot_general(
                qd[rows], ch["st"][i].astype(BF16), nt, preferred_element_type=F32)
    return outs


def _hgrn_kernel(q_ref, kf_ref, lf_ref, kb_ref, lb_ref, v_ref,
                 ckf_ref, clf_ref, ckb_ref, clb_ref, cv_ref, gn_ref, o_ref,
                 st_ref, of_ref, ob_ref):
    c = HG_CHUNK
    n_heads = q_ref.shape[1] // HG_DK
    n_ctx = ckf_ref.shape[0] // c
    n_lat = q_ref.shape[0] // c
    row = lax.broadcasted_iota(I32, (c, c), 0)
    col = lax.broadcasted_iota(I32, (c, c), 1)
    tri = {False: jnp.where(col <= row, 1.0, 0.0).astype(BF16),
           True: jnp.where(col >= row, 1.0, 0.0).astype(BF16)}
    tri = {rev: jnp.concatenate([t, t], axis=1) for rev, t in tri.items()}
    rows2 = lax.broadcasted_iota(I32, (HG_STACK * c, HG_STACK * c), 0)
    cols2 = lax.broadcasted_iota(I32, (HG_STACK * c, HG_STACK * c), 1)
    x = rows2 ^ cols2
    lvl = jnp.where(x == 0, len(_HG_LEVELS), -1)
    for li, h in enumerate(_HG_LEVELS):
        lvl = jnp.where((x >= h) & (x < 2 * h), li, lvl)
    level_id = {False: jnp.where(cols2 <= rows2, lvl, -1), True: jnp.where(cols2 >= rows2, lvl, -1)}

    st_ref[...] = jnp.zeros_like(st_ref)

    def dirs_at(a, z, k_f, l_f, k_b, l_b, v, q):
        out = []
        for slot, (rev, start, kk, ll) in enumerate(((False, a, k_f, l_f), (True, z, k_b, l_b))):
            rows = pl.ds(start, c)
            out.append(dict(reverse=rev, g=ll[rows, :], k=kk[rows, :], v=v[rows, :],
                            q=None if q is None else q[rows, :],
                            st_ref=st_ref.at[slot]))
        return out

    def ctx_body(j, carry):
        a = pl.multiple_of(j * c, c)
        z = pl.multiple_of((n_ctx - 1 - j) * c, c)
        _hgrn_chunks(dirs_at(a, z, ckf_ref, clf_ref, ckb_ref, clb_ref, cv_ref, None),
                     tri, level_id, False)
        return carry

    lax.fori_loop(0, n_ctx, ctx_body, 0)

    def lat_body(j, carry):
        a = pl.multiple_of(j * c, c)
        z = pl.multiple_of((n_lat - 1 - j) * c, c)
        outs = _hgrn_chunks(dirs_at(a, z, kf_ref, lf_ref, kb_ref, lb_ref, v_ref, q_ref),
                            tri, level_id, True)
        for hd in range(n_heads):
            lanes = slice(hd * HG_DV, (hd + 1) * HG_DV)
            of_ref[pl.ds(a, c), lanes] = outs[0][hd]
            ob_ref[pl.ds(z, c), lanes] = outs[1][hd]
        return carry

    lax.fori_loop(0, n_lat, lat_body, 0)

    for hd in range(n_heads):
        lanes = slice(hd * HG_DV, (hd + 1) * HG_DV)
        o = of_ref[:, lanes] + ob_ref[:, lanes]
        ms = jnp.mean(o * o, axis=-1, keepdims=True)
        o_ref[:, lanes] = (o * lax.rsqrt(ms + NORM_EPS) * gn_ref[:, lanes]).astype(BF16)


def _hgrn(hq, kf, lf, kb, lb, vi, ckf, clf, ckb, clb, cvi, gn):
    b, seq, _ = hq.shape
    n_ctx = ckf.shape[1]
    nh = HG_HEADS_PER_STEP
    lat = pl.BlockSpec((None, seq, nh * HG_DK), lambda i, h: (i, 0, h))
    cx = pl.BlockSpec((None, n_ctx, nh * HG_DK), lambda i, h: (i, 0, h))
    return pl.pallas_call(
        _hgrn_kernel,
        grid=(b, HG_HEADS // nh),
        in_specs=[lat, lat, lat, lat, lat, lat, cx, cx, cx, cx, cx,
                  pl.BlockSpec((1, nh * HG_DV), lambda i, h: (0, h))],
        out_specs=lat,
        out_shape=jax.ShapeDtypeStruct((b, seq, HG_VAL), BF16),
        scratch_shapes=[pltpu.VMEM((2, nh, HG_DV, HG_DK), F32),
                        pltpu.VMEM((seq, nh * HG_DV), F32), pltpu.VMEM((seq, nh * HG_DV), F32)],
        compiler_params=_params(("parallel", "parallel")),
        name="hgrn_scan",
    )(hq, kf, lf, kb, lb, vi, ckf, clf, ckb, clb, cvi, gn)


_HEADS_PER_TILE = LANES // ATT_HEAD_DIM


def _attn_kernel(q_ref, kx_ref, vx_ref, kc_ref, vc_ref, o_ref, kp_ref, vp_ref):
    kvh = pl.program_id(1)

    @pl.when(pl.program_id(2) == 0)
    def _build():
        k_all = jnp.concatenate([kc_ref[...], kx_ref[...]], axis=0)
        v_all = jnp.concatenate([vc_ref[...], vx_ref[...]], axis=0)
        r = lax.broadcasted_iota(I32, (ATT_KV, LANES), 0)
        cidx = lax.broadcasted_iota(I32, (ATT_KV, LANES), 1)
        for u in range(_HEADS_PER_TILE):
            place = (cidx // ATT_HEAD_DIM == u) & (r == kvh * ATT_HEAD_DIM + cidx - u * ATT_HEAD_DIM)
            rep = jnp.where(place, 1.0, 0.0).astype(BF16)
            kp_ref[u] = jnp.dot(k_all, rep, preferred_element_type=F32).astype(BF16)
            vp_ref[u] = jnp.dot(v_all, rep, preferred_element_type=F32).astype(BF16)

    units = [(slice(t * Q_TILE, (t + 1) * Q_TILE), slice(pair * LANES, (pair + 1) * LANES), u)
             for t in range(q_ref.shape[0] // Q_TILE)
             for pair in range(ATT_GROUPS // _HEADS_PER_TILE)
             for u in range(_HEADS_PER_TILE)]

    def scores(unit):
        rows, lanes, u = unit
        return lax.dot_general(q_ref[rows, lanes], kp_ref[u], (((1,), (1,)), ((), ())),
                               preferred_element_type=F32)

    s_next = scores(units[0])
    acc = None
    for i, (rows, lanes, u) in enumerate(units):
        s = s_next
        if i + 1 < len(units):
            s_next = scores(units[i + 1])
        p = jnp.exp2(s - jnp.max(s, axis=-1, keepdims=True))
        inv = 1.0 / jnp.sum(p, axis=-1, keepdims=True)
        part = jnp.dot(p.astype(BF16), vp_ref[u], preferred_element_type=F32) * inv
        acc = part if u == 0 else acc + part
        if u == _HEADS_PER_TILE - 1:
            o_ref[rows, lanes] = acc.astype(BF16)


def _attention(aq, ak, av, cak, cav):
    b, seq, _ = aq.shape
    n_ctx = cak.shape[1]
    width = ATT_GROUPS * ATT_HEAD_DIM
    kx = pl.BlockSpec((None, seq, ATT_KV), lambda i, h, t: (i, 0, 0))
    kc = pl.BlockSpec((None, n_ctx, ATT_KV), lambda i, h, t: (i, 0, 0))
    rows = Q_TILE * Q_TILES_PER_STEP
    assert seq % rows == 0
    qo = pl.BlockSpec((None, rows, width), lambda i, h, t: (i, t, h))
    return pl.pallas_call(
        _attn_kernel,
        grid=(b, ATT_KV_HEADS, seq // rows),
        in_specs=[qo, kx, kx, kc, kc],
        out_specs=qo,
        out_shape=jax.ShapeDtypeStruct((b, seq, ATT_Q), BF16),
        scratch_shapes=[pltpu.VMEM((_HEADS_PER_TILE, seq + n_ctx, LANES), BF16),
                        pltpu.VMEM((_HEADS_PER_TILE, seq + n_ctx, LANES), BF16)],
        compiler_params=_params(("parallel", "parallel", "arbitrary")),
        name="attention",
    )(aq, ak, av, cak, cav)


def _merge_kernel(on_ref, sg_ref, oa_ref, ga_ref, gb_ref, x_ref, mod_ref, n2_ref,
                  wa_ref, wb_ref, wo_ref, wr_ref, x1_ref, h2_ref, lg_ref):
    tm = x_ref.shape[0]
    parts = [slice(i * tm // MERGE_SUBTILES, (i + 1) * tm // MERGE_SUBTILES)
             for i in range(MERGE_SUBTILES)]
    a, bb, y = {}, {}, {}
    for i, rows in enumerate(parts):
        a_in = (on_ref[rows, :].astype(F32) * sg_ref[rows, :].astype(F32)).astype(BF16)
        a[i] = jnp.dot(a_in, wa_ref[...], preferred_element_type=F32)
        bb[i] = jnp.dot(oa_ref[rows, :], wb_ref[...], preferred_element_type=F32)
    for i, rows in enumerate(parts):
        m = ga_ref[rows, :].astype(F32) * a[i] + gb_ref[rows, :].astype(F32) * bb[i]
        y[i] = jnp.dot(m.astype(BF16), wo_ref[...], preferred_element_type=F32)
    wr = wr_ref[...]
    for i, rows in enumerate(parts):
        x1 = x_ref[rows, :] + mod_ref[2:3, :] * y[i]
        x1_ref[rows, :] = x1
        h2 = _rms_mod(x1, n2_ref[...], mod_ref[3:4, :], mod_ref[4:5, :])
        h_hi = h2.astype(BF16)
        h_lo = (h2 - h_hi.astype(F32)).astype(BF16)
        h2_ref[rows, :] = h_hi
        r = (jnp.dot(h_hi, wr, preferred_element_type=F32)
             + jnp.dot(h_lo, wr, preferred_element_type=F32))
        lg_ref[rows, :] = r[:, :N_EXPERTS] + r[:, N_EXPERTS:2 * N_EXPERTS]


def _merge(on, sg, oatt, sga, sgb, x, mods, norm2_g, wa, wb, wo, wr):
    b, seq, d = x.shape
    tm = TOKEN_TILE
    tok = lambda n: pl.BlockSpec((None, tm, n), lambda i, t: (i, t, 0))
    return pl.pallas_call(
        _merge_kernel,
        grid=(b, seq // tm),
        in_specs=[tok(HG_VAL), tok(HG_VAL), tok(ATT_Q), tok(d), tok(d), tok(d),
                  pl.BlockSpec((None, N_MOD, d), lambda i, t: (i, 0, 0)),
                  _const_spec((1, d)), _const_spec(wa.shape), _const_spec(wb.shape),
                  _const_spec(wo.shape), _const_spec(wr.shape)],
        out_specs=[tok(d), tok(d), tok(N_EXPERTS)],
        out_shape=[jax.ShapeDtypeStruct((b, seq, d), F32),
                   jax.ShapeDtypeStruct((b, seq, d), BF16),
                   jax.ShapeDtypeStruct((b, seq, N_EXPERTS), F32)],
        compiler_params=_params(("parallel", "parallel")),
        name="merge",
    )(on, sg, oatt, sga, sgb, x, mods, norm2_g, wa, wb, wo, wr)


def _route_kernel(lg_ref, pos_ref, gate_ref, starts_ref, *, cap):
    lg = lg_ref[...]
    n_b, n_exp, length = lg.shape
    e = jnp.exp(lg - jnp.max(lg, axis=1, keepdims=True))
    aff = (e / jnp.sum(e, axis=1, keepdims=True)).reshape(n_b * n_exp, length)
    n_e = n_b * n_exp
    def count(mask):
        return jnp.sum(jnp.where(mask, 1.0, 0.0), axis=1, keepdims=True)

    def value_step(i, t):
        cand = t | lax.shift_left(jnp.int32(1), jnp.asarray(30 - i, I32))
        enough = count(aff >= lax.bitcast_convert_type(cand, F32)) >= cap
        return jnp.where(enough, cand, t)

    thr = lax.bitcast_convert_type(
        lax.fori_loop(0, 31, value_step, jnp.zeros((n_e, 1), I32)), F32)
    above = aff > thr
    tied = aff == thr
    need = cap - count(above)
    idx = lax.broadcasted_iota(I32, (1, length), 1)
    n_bits = int(np.log2(length))

    def index_step(i, j):
        cand = j | lax.shift_left(jnp.int32(1), jnp.asarray(n_bits - 1 - i, I32))
        return jnp.where(count(tied & (idx < cand)) < need, cand, j)

    last = lax.fori_loop(0, n_bits, index_step, jnp.zeros((n_e, 1), I32))
    sel = above | (tied & (idx <= last))

    r = lax.broadcasted_iota(I32, (LANES, LANES), 0)
    cc = lax.broadcasted_iota(I32, (LANES, LANES), 1)
    before = jnp.where(r < cc, 1.0, 0.0).astype(BF16)
    sel_b = jnp.where(sel, 1.0, 0.0).astype(BF16)
    offset = jnp.zeros((n_e, 1), F32)
    pieces = []
    groups_per_tile = ROUTE_TILE // LANES
    lane = lax.broadcasted_iota(I32, (1, LANES), 1)
    starts = jnp.zeros((n_e, LANES), F32)
    for gidx in range(length // LANES):
        blk = sel_b[:, gidx * LANES:(gidx + 1) * LANES]
        pieces.append(jnp.dot(blk, before, preferred_element_type=F32) + offset)
        offset = offset + jnp.sum(blk.astype(F32), axis=1, keepdims=True)
        if (gidx + 1) % groups_per_tile == 0:
            starts = jnp.where(lane == (gidx + 1) // groups_per_tile, offset, starts)
    pos = jnp.concatenate(pieces, axis=1)
    pos_ref[...] = jnp.where(sel, pos.astype(I32), -1).reshape(n_b, n_exp, length)
    gate_ref[...] = jnp.where(sel, aff, 0.0).reshape(n_b, n_exp, length)
    starts_ref[...] = starts.astype(I32).reshape(n_b, n_exp, LANES)


def _route(logits_t, cap):
    b, n_e, length = logits_t.shape
    assert length // ROUTE_TILE < LANES and n_e % SUBLANES == 0
    spec = _const_spec((b, n_e, length))
    return pl.pallas_call(
        functools.partial(_route_kernel, cap=cap),
        grid=(1,),
        in_specs=[spec],
        out_specs=[spec, spec, _const_spec((b, n_e, LANES))],
        out_shape=[jax.ShapeDtypeStruct((b, n_e, length), I32),
                   jax.ShapeDtypeStruct((b, n_e, length), F32),
                   jax.ShapeDtypeStruct((b, n_e, LANES), I32)],
        compiler_params=_params(("arbitrary",)),
        name="route",
    )(logits_t)


def _tile_windows(starts_ref, sample, tile, n_e, n_tiles):
    stride = n_tiles + 1
    lows, n_win = [], jnp.int32(0)
    for e in range(n_e):
        at = (sample * n_e + e) * stride + tile
        low = (starts_ref[at] // BF16_ROWS) * BF16_ROWS
        lows.append(low)
        n_win = jnp.maximum(n_win, (starts_ref[at + 1] - low + ROUTE_WINDOW - 1) // ROUTE_WINDOW)
    return lows, n_win


def _window(low, w, cap):
    lo = low + w * ROUTE_WINDOW
    start = pl.multiple_of(jnp.minimum(lo, cap - ROUTE_WINDOW), BF16_ROWS)
    slot = start + lax.broadcasted_iota(I32, (ROUTE_WINDOW, 1), 0)
    return start, slot, slot >= lo


def _gather_kernel(starts_ref, pos_ref, h_ref, xe_ref):
    n_e, cap, d = xe_ref.shape
    n_tiles = h_ref.shape[0] // ROUTE_TILE
    sample = pl.program_id(0)
    xe_ref[...] = jnp.zeros_like(xe_ref)

    def window(lows, tokens, w):
        pieces, begins = [], []
        for e in range(n_e):
            start, slot, live = _window(lows[e], w, cap)
            hit = (pos_ref[e:e + 1, tokens] == slot) & live
            pieces.append(jnp.where(hit, 1.0, 0.0).astype(BF16))
            begins.append(start)
        onehot = jnp.concatenate(pieces, axis=0)
        rows = jnp.dot(onehot, h_ref[tokens, :], preferred_element_type=F32).astype(BF16)
        return begins, rows

    def accumulate(begins, rows):
        for e in range(n_e):
            dst = xe_ref.at[e, pl.ds(begins[e], ROUTE_WINDOW), :]
            dst[...] = dst[...] + rows[e * ROUTE_WINDOW:(e + 1) * ROUTE_WINDOW]

    tiles = []
    for j in range(n_tiles):
        tokens = slice(j * ROUTE_TILE, (j + 1) * ROUTE_TILE)
        lows, n_win = _tile_windows(starts_ref, sample, j, n_e, n_tiles)
        tiles.append((tokens, lows, n_win))
        accumulate(*window(lows, tokens, 0))
    for tokens, lows, n_win in tiles:
        def more(w, carry, tokens=tokens, lows=lows):
            accumulate(*window(lows, tokens, w))
            return carry

        lax.fori_loop(1, n_win, more, 0)


def _gather(starts, pos, h2, cap):
    b, n_e, length = pos.shape
    d = h2.shape[-1]
    assert length % ROUTE_TILE == 0 and cap % BF16_ROWS == 0 and cap >= ROUTE_WINDOW
    grid_spec = pltpu.PrefetchScalarGridSpec(
        num_scalar_prefetch=1,
        grid=(b,),
        in_specs=[pl.BlockSpec((None, n_e, length), lambda i, st: (i, 0, 0)),
                  pl.BlockSpec((None, length, d), lambda i, st: (i, 0, 0))],
        out_specs=pl.BlockSpec((None, n_e, cap, d), lambda i, st: (i, 0, 0, 0)))
    return pl.pallas_call(
        _gather_kernel,
        grid_spec=grid_spec,
        out_shape=jax.ShapeDtypeStruct((b, n_e, cap, d), BF16),
        compiler_params=_params(("parallel",)),
        name="gather",
    )(starts, pos, h2)


def _ffn_kernel(xe_ref, wg_ref, wu_ref, wd_ref, ye_ref):
    nb, cap, d = xe_ref.shape
    x = xe_ref[...].reshape(nb * cap, d)
    hg = jnp.dot(x, wg_ref[...].astype(BF16), preferred_element_type=F32)
    hu = jnp.dot(x, wu_ref[...].astype(BF16), preferred_element_type=F32)
    hid = (hg * jax.nn.sigmoid(hg) * hu).astype(BF16)
    ye = jnp.dot(hid, wd_ref[...].astype(BF16), preferred_element_type=F32)
    ye_ref[...] = ye.astype(BF16).reshape(nb, cap, d)


def _ffn(xe, wg, wu, wd):
    b, n_e, cap, d = xe.shape
    ff = wg.shape[-1]
    nb = int(np.gcd(b, FFN_BATCH_GROUP))
    tok = pl.BlockSpec((nb, None, cap, d), lambda e, i: (i, e, 0, 0))
    return pl.pallas_call(
        _ffn_kernel,
        grid=(n_e, b // nb),
        in_specs=[tok,
                  pl.BlockSpec((None, d, ff), lambda e, i: (e, 0, 0)),
                  pl.BlockSpec((None, d, ff), lambda e, i: (e, 0, 0)),
                  pl.BlockSpec((None, ff, d), lambda e, i: (e, 0, 0))],
        out_specs=tok,
        out_shape=jax.ShapeDtypeStruct((b, n_e, cap, d), BF16),
        compiler_params=_params(("parallel", "arbitrary")),
        name="expert_ffn",
    )(xe, wg, wu, wd)


def _combine_kernel(starts_ref, pos_ref, gate_ref, ye_ref, x1_ref, mod_ref, fg_ref, o_ref):
    n_e, cap, _ = ye_ref.shape
    n_sub = x1_ref.shape[0] // ROUTE_TILE
    n_tiles = pl.num_programs(1) * n_sub

    def window(lows, tokens, w):
        pieces, rows = [], []
        for e in range(n_e):
            start, slot, live = _window(lows[e], w, cap)
            hit = (pos_ref[e:e + 1, tokens] == slot) & live
            pieces.append(jnp.where(hit, gate_ref[e:e + 1, tokens], 0.0).astype(BF16))
            rows.append(ye_ref[e, pl.ds(start, ROUTE_WINDOW), :])
        scatter = jnp.concatenate(pieces, axis=0)
        return lax.dot_general(scatter, jnp.concatenate(rows, axis=0),
                               (((0,), (0,)), ((), ())), preferred_element_type=F32)

    def finish(tokens, y):
        x2 = x1_ref[tokens, :] + mod_ref[5:6, :] * y
        ms = jnp.mean(x2 * x2, axis=-1, keepdims=True)
        o_ref[tokens, :] = x2 * lax.rsqrt(ms + NORM_EPS) * fg_ref[...]

    subs = []
    for s in range(n_sub):
        tokens = slice(s * ROUTE_TILE, (s + 1) * ROUTE_TILE)
        lows, n_win = _tile_windows(starts_ref, pl.program_id(0), pl.program_id(1) * n_sub + s,
                                    n_e, n_tiles)
        subs.append((tokens, lows, n_win, window(lows, tokens, 0)))
    for tokens, lows, n_win, y in subs:
        finish(tokens, y)
    for tokens, lows, n_win, y in subs:
        @pl.when(n_win > 1)
        def _more(tokens=tokens, lows=lows, n_win=n_win, y=y):
            finish(tokens, lax.fori_loop(1, n_win, lambda w, a: a + window(lows, tokens, w), y))


def _combine(starts, pos, gate, ye, x1, mods, final_g):
    b, seq, d = x1.shape
    n_e, cap = ye.shape[1], ye.shape[2]
    tm = ROUTE_TILE * COMBINE_SUBTILES
    tok = lambda n: pl.BlockSpec((None, tm, n), lambda i, t, st: (i, t, 0))
    sel = pl.BlockSpec((None, n_e, tm), lambda i, t, st: (i, 0, t))
    grid_spec = pltpu.PrefetchScalarGridSpec(
        num_scalar_prefetch=1,
        grid=(b, seq // tm),
        in_specs=[sel, sel,
                  pl.BlockSpec((None, n_e, cap, d), lambda i, t, st: (i, 0, 0, 0)),
                  tok(d),
                  pl.BlockSpec((None, N_MOD, d), lambda i, t, st: (i, 0, 0)),
                  pl.BlockSpec((1, d), lambda i, t, st: (0, 0))],
        out_specs=tok(d))
    return pl.pallas_call(
        _combine_kernel,
        grid_spec=grid_spec,
        out_shape=jax.ShapeDtypeStruct((b, seq, d), F32),
        compiler_params=_params(("parallel", "arbitrary")),
        name="combine",
    )(starts, pos, gate, ye, x1, mods, final_g)


def _rope_tables(length):
    rows = length // GRID_W
    row = jnp.repeat(jnp.arange(rows, dtype=F32), GRID_W)
    col = jnp.tile(jnp.arange(GRID_W, dtype=F32), rows)
    inv_freq = ROPE_THETA ** (-jnp.arange(0, ROPE_AXIS_DIM, 2, dtype=F32) / ROPE_AXIS_DIM)
    half = ROPE_AXIS_DIM // 2
    cos_parts, sin_parts = [], []
    for pos in (row, col):
        ang = pos[:, None] * inv_freq
        cos_parts += [jnp.cos(ang), jnp.cos(ang)]
        sin_parts += [-jnp.sin(ang), jnp.sin(ang)]
    cos = jnp.concatenate(cos_parts, axis=-1)
    sin = jnp.concatenate(sin_parts, axis=-1)
    reps = LANES // ATT_HEAD_DIM
    assert half * 4 == ATT_HEAD_DIM
    return jnp.tile(cos, (1, reps)), jnp.tile(sin, (1, reps))


def kernel(x, c, ctx, c_ctx, w_mod, b_mod, norm1_g, norm2_g, w_in, hg_lb_logits, hg_norm_g,
           q_norm_g, k_norm_g, w_branch_a, w_branch_b, w_out, w_router, w_exp_gate, w_exp_up,
           w_exp_down, final_norm_g):
    b, seq, d = x.shape
    depth = w_mod.shape[0]
    assert depth == 1, "context-stream update between layers is not implemented"
    layer = 0
    cap = EC_CAPACITY_FACTOR * seq // N_EXPERTS

    n_rows = -(-(b + 1) // SUBLANES) * SUBLANES
    cond = jnp.zeros((n_rows, d), F32).at[:b].set(c).at[b].set(c_ctx)
    mods = _adaln(cond, w_mod[layer], b_mod[layer]).reshape(n_rows, N_MOD, d)

    cos, sin = _rope_tables(seq)
    gsum = jnp.asarray(np.kron(np.eye(ATT_HEADS), np.ones((ATT_HEAD_DIM, ATT_HEAD_DIM))), BF16)
    qg = jnp.tile(q_norm_g[layer], ATT_HEADS).reshape(1, ATT_Q)
    kg = jnp.tile(k_norm_g[layer], ATT_KV_HEADS).reshape(1, ATT_KV)
    w_bf = w_in[layer].astype(BF16)
    w_ctx = jnp.concatenate([w_bf[:, _OFF_FF:_OFF_G], w_bf[:, _OFF_AK:_OFF_AV + ATT_KV]], axis=1)
    n1 = norm1_g[layer].reshape(1, d)

    hq, kf, lf, kb, lb, vi, sg, aq, ak, av, sga, sgb = _inproj_latent(
        x, mods, n1, w_bf, hg_lb_logits, qg, kg, cos, sin, gsum, layer)
    ckf, clf, ckb, clb, cvi, cak, cav = _inproj_context(
        ctx, mods, b, n1, w_ctx, hg_lb_logits, kg, gsum[:ATT_KV, :ATT_KV], layer)

    on = _hgrn(hq, kf, lf, kb, lb, vi, ckf, clf, ckb, clb, cvi,
               hg_norm_g[layer].reshape(1, HG_VAL))
    oatt = _attention(aq, ak, av, cak, cav)

    wr_hi = w_router[layer].astype(BF16)
    wr_lo = (w_router[layer] - wr_hi.astype(F32)).astype(BF16)
    wr = jnp.zeros((d, LANES), BF16).at[:, :N_EXPERTS].set(wr_hi)
    wr = wr.at[:, N_EXPERTS:2 * N_EXPERTS].set(wr_lo)
    x1, h2, logits = _merge(on, sg, oatt, sga, sgb, x, mods, norm2_g[layer].reshape(1, d),
                            w_branch_a[layer].astype(BF16), w_branch_b[layer].astype(BF16),
                            w_out[layer].astype(BF16), wr)

    pos, gate, starts = _route(jnp.swapaxes(logits, 1, 2), cap)
    starts = starts[:, :, :seq // ROUTE_TILE + 1].reshape(-1)
    xe = _gather(starts, pos, h2, cap)
    ye = _ffn(xe, w_exp_gate[layer], w_exp_up[layer], w_exp_down[layer])
    return _combine(starts, pos, gate, ye, x1, mods, final_norm_g.reshape(1, d))
```
